```python
import math
import jax
import jax.numpy as jnp
from jax import lax
import numpy as np

D_MODEL = 1024
BATCH = 8
SEQ = 2048
DEPTH = 2

N_A_LAYERS = DEPTH // 2
N_B_LAYERS = DEPTH - N_A_LAYERS
SSM_WIDTH = D_MODEL
GROUP = 16
N_GROUPS = SSM_WIDTH // GROUP
STATE = 64
DT_MIN = 1e-3
DT_MAX = 1e-1
N_HEADS = 16
HEAD_DIM = D_MODEL // N_HEADS
ATTN_WIDTH = N_HEADS * HEAD_DIM
Q_BLOCK = 128
EPS = 1e-6

kernel_name = "yoco_s5_fox_adaln_hybrid"


def _rms(x, g):
    xf = x.astype(jnp.float32)
    y = xf * lax.rsqrt(jnp.mean(xf * xf, axis=-1, keepdims=True) + EPS)
    return (y * g.astype(jnp.float32)).astype(x.dtype)


def _modulation(c, w, b, n):
    m = jax.nn.silu(c) @ w + b
    return jnp.split(m[:, None, :], n, axis=-1)


def _cplx_combine(e1, e2):
    a1r, a1i, b1r, b1i = e1
    a2r, a2i, b2r, b2i = e2
    ar = a1r * a2r - a1i * a2i
    ai = a1r * a2i + a1i * a2r
    br = a2r * b1r - a2i * b1i + b2r
    bi = a2r * b1i + a2i * b1r + b2i
    return (ar, ai, br, bi)


def _s5_mixer(h, w_in, log_dt, A_re, A_im, B_re, B_im, C_re, C_im, D, w_glu, b_glu, w_out):
    bsz, L, _ = h.shape
    u, z = jnp.split(h @ w_in, 2, axis=-1)
    f32 = jnp.float32
    dt = jnp.exp(log_dt.astype(f32))[:, None]
    ar, ai = A_re.astype(f32), A_im.astype(f32)
    mag = jnp.exp(ar * dt)
    abar_r, abar_i = mag * jnp.cos(ai * dt), mag * jnp.sin(ai * dt)
    den = ar * ar + ai * ai
    nr = abar_r - 1.0
    coef_r = (nr * ar + abar_i * ai) / den
    coef_i = (abar_i * ar - nr * ai) / den
    br, bi = B_re.astype(f32), B_im.astype(f32)
    bb_r = coef_r[..., None] * br - coef_i[..., None] * bi
    bb_i = coef_r[..., None] * bi + coef_i[..., None] * br
    ug = u.astype(f32).reshape(bsz, L, N_GROUPS, GROUP)
    bu_r = jnp.einsum('blgc,gpc->blgp', ug, bb_r)
    bu_i = jnp.einsum('blgc,gpc->blgp', ug, bb_i)
    a_r = jnp.broadcast_to(abar_r[None, None], (1, L, N_GROUPS, STATE))
    a_i = jnp.broadcast_to(abar_i[None, None], (1, L, N_GROUPS, STATE))
    _, _, s_r, s_i = lax.associative_scan(_cplx_combine, (a_r, a_i, bu_r, bu_i), axis=1)
    y = (jnp.einsum('blgp,gcp->blgc', s_r, C_re.astype(f32))
         - jnp.einsum('blgp,gcp->blgc', s_i, C_im.astype(f32)))
    y = y.reshape(bsz, L, SSM_WIDTH) + D.astype(f32) * u.astype(f32)
    y = jax.nn.gelu(y)
    y = y * jax.nn.sigmoid(y @ w_glu.astype(f32) + b_glu.astype(f32))
    y = y * jax.nn.silu(z.astype(f32))
    return y.astype(h.dtype) @ w_out


def _head_rms(t, g):
    tf = t.astype(jnp.float32)
    return tf * lax.rsqrt(jnp.mean(tf * tf, axis=-1, keepdims=True) + EPS) * g.astype(jnp.float32)


def _shared_kv(x, c, g, mod_w, mod_b, kv_w, f_bias, k_norm_g):
    bsz, L, _ = x.shape
    shift, scale = _modulation(c, mod_w, mod_b, 2)
    h = _rms(x, g) * (1.0 + scale) + shift
    kvf = h @ kv_w
    k = kvf[..., :ATTN_WIDTH].reshape(bsz, L, N_HEADS, HEAD_DIM)
    v = kvf[..., ATTN_WIDTH:2 * ATTN_WIDTH].reshape(bsz, L, N_HEADS, HEAD_DIM)
    f_logit = kvf[..., 2 * ATTN_WIDTH:].astype(jnp.float32) + f_bias.astype(jnp.float32)
    k = _head_rms(k, k_norm_g)
    F = jnp.cumsum(jax.nn.log_sigmoid(f_logit), axis=1)
    return k, v, F


def _fox_mixer(h, k, v, F, w_in, q_norm_g, w_out):
    bsz, L, _ = h.shape
    nblk = L // Q_BLOCK
    q, z = jnp.split(h @ w_in, 2, axis=-1)
    q = _head_rms(q.reshape(bsz, L, N_HEADS, HEAD_DIM), q_norm_g) * (HEAD_DIM ** -0.5)
    qb = q.reshape(bsz, nblk, Q_BLOCK, N_HEADS, HEAD_DIM).transpose(1, 0, 3, 2, 4)
    fq = F.reshape(bsz, nblk, Q_BLOCK, N_HEADS).transpose(1, 0, 3, 2)
    kt = k.transpose(0, 2, 1, 3)
    vt = v.astype(jnp.float32).transpose(0, 2, 1, 3)
    fk = F.transpose(0, 2, 1)
    kpos = jnp.arange(L)

    def one_block(args):
        qi, fqi, i = args
        s = jnp.einsum('bhqd,bhkd->bhqk', qi, kt)
        s = s + fqi[..., None] - fk[:, :, None, :]
        qpos = i * Q_BLOCK + jnp.arange(Q_BLOCK)
        s = jnp.where(kpos[None, :] <= qpos[:, None], s, -jnp.inf)
        p = jax.nn.softmax(s, axis=-1)
        return jnp.einsum('bhqk,bhkd->bhqd', p, vt)

    o = lax.map(one_block, (qb, fq, jnp.arange(nblk)))
    o = o.transpose(1, 0, 3, 2, 4).reshape(bsz, L, ATTN_WIDTH)
    o = o * jax.nn.silu(z.astype(jnp.float32))
    return o.astype(h.dtype) @ w_out


def setup_inputs(seed: int = 0) -> dict:
    key = jax.random.key(seed)
    ks = iter(jax.random.split(key, 40))
    f32 = jnp.float32

    def nrm(shape, scale):
        return scale * jax.random.normal(next(ks), shape, f32)

    D, E, G, P, NA, NB = D_MODEL, SSM_WIDTH, N_GROUPS, STATE, N_A_LAYERS, N_B_LAYERS
    AW, H = ATTN_WIDTH, N_HEADS
    inp = {}
    inp['x'] = nrm((BATCH, SEQ, D), 1.0)
    inp['c'] = nrm((BATCH, D), 1.0)
    inp['a_norm_g'] = 1.0 + nrm((NA, D), 0.02)
    inp['a_mod_w'] = nrm((NA, D, 3 * D), 0.5 * D ** -0.5)
    inp['a_mod_b'] = nrm((NA, 3 * D), 0.02)
    inp['a_w_in'] = nrm((NA, D, 2 * E), D ** -0.5)
    inp['a_log_dt'] = jax.random.uniform(next(ks), (NA, G), f32, math.log(DT_MIN), math.log(DT_MAX))
    inp['a_A_re'] = -0.5 + nrm((NA, G, P), 0.01)
    inp['a_A_im'] = math.pi * jnp.broadcast_to(jnp.arange(P, dtype=f32), (NA, G, P)) + nrm((NA, G, P), 0.01)
    inp['a_B_re'] = nrm((NA, G, P, GROUP), (2 * GROUP) ** -0.5)
    inp['a_B_im'] = nrm((NA, G, P, GROUP), (2 * GROUP) ** -0.5)
    inp['a_C_re'] = nrm((NA, G, GROUP, P), 0.5)
    inp['a_C_im'] = nrm((NA, G, GROUP, P), 0.5)
    inp['a_D'] = nrm((NA, E), 1.0)
    inp['a_w_glu'] = nrm((NA, E, E), E ** -0.5)
    inp['a_b_glu'] = nrm((NA, E), 0.02)
    inp['a_w_out'] = nrm((NA, E, D), E ** -0.5)
    inp['kv_norm_g'] = 1.0 + nrm((D,), 0.02)
    inp['kv_mod_w'] = nrm((D, 2 * D), 0.5 * D ** -0.5)
    inp['kv_mod_b'] = nrm((2 * D,), 0.02)
    inp['kv_w'] = nrm((D, 2 * AW + H), D ** -0.5)
    inp['kv_f_bias'] = jax.random.uniform(next(ks), (H,), f32, 1.0, 4.0)
    inp['k_norm_g'] = 1.0 + nrm((HEAD_DIM,), 0.02)
    inp['b_norm_g'] = 1.0 + nrm((NB, D), 0.02)
    inp['b_mod_w'] = nrm((NB, D, 3 * D), 0.5 * D ** -0.5)
    inp['b_mod_b'] = nrm((NB, 3 * D), 0.02)
    inp['b_w_in'] = nrm((NB, D, 2 * AW), D ** -0.5)
    inp['q_norm_g'] = 1.0 + nrm((NB, HEAD_DIM), 0.02)
    inp['b_w_out'] = nrm((NB, AW, D), AW ** -0.5)
    return inp


def reference(x, c, a_norm_g, a_mod_w, a_mod_b, a_w_in, a_log_dt, a_A_re, a_A_im,
              a_B_re, a_B_im, a_C_re, a_C_im, a_D, a_w_glu, a_b_glu, a_w_out,
              kv_norm_g, kv_mod_w, kv_mod_b, kv_w, kv_f_bias, k_norm_g,
              b_norm_g, b_mod_w, b_mod_b, b_w_in, q_norm_g, b_w_out):
    k = v = F = None
    for layer in range(DEPTH):
        if layer < N_A_LAYERS:
            i = layer
            shift, scale, gate = _modulation(c, a_mod_w[i], a_mod_b[i], 3)
            h = _rms(x, a_norm_g[i]) * (1.0 + scale) + shift
            y = _s5_mixer(h, a_w_in[i], a_log_dt[i], a_A_re[i], a_A_im[i], a_B_re[i], a_B_im[i],
                          a_C_re[i], a_C_im[i], a_D[i], a_w_glu[i], a_b_glu[i], a_w_out[i])
            x = x + gate * y
        else:
            if layer == N_A_LAYERS:
                k, v, F = _shared_kv(x, c, kv_norm_g, kv_mod_w, kv_mod_b, kv_w, kv_f_bias, k_norm_g)
            j = layer - N_A_LAYERS
            shift, scale, gate = _modulation(c, b_mod_w[j], b_mod_b[j], 3)
            h = _rms(x, b_norm_g[j]) * (1.0 + scale) + shift
            y = _fox_mixer(h, k, v, F, b_w_in[j], q_norm_g[j], b_w_out[j])
            x = x + gate * y
    return x
```

```python
import functools

import jax
import jax.numpy as jnp
from jax import lax
from jax.experimental import pallas as pl
from jax.experimental.pallas import tpu as pltpu

D_MODEL = 1024
BATCH = 8
SEQ = 2048
GROUP = 16
N_GROUPS = D_MODEL // GROUP
STATE = 64
N_STATES = N_GROUPS * STATE
N_HEADS = 16
HEAD_DIM = 64
EPS = 1e-6

F32 = jnp.float32
BF16 = jnp.bfloat16

SUBLANES = 8
LANES = 128
MXU_DIM = 256
VMEM_LIMIT_BYTES = 56 * 1024 * 1024

S5_T = 32
SCAN_W = 512
N_SCAN = N_STATES // SCAN_W
S5_KC = MXU_DIM
N_KC = D_MODEL // S5_KC
KC_STATES = (S5_KC // GROUP) * STATE
PROJ_TM = 512
ATT_T = 512
N_KT = SEQ // ATT_T
BIAS_LANE = HEAD_DIM
MOD_BN = 512


def _cparams(sem):
    return pltpu.CompilerParams(dimension_semantics=sem, vmem_limit_bytes=VMEM_LIMIT_BYTES)


def _const_spec(shape):
    nd = len(shape)
    return pl.BlockSpec(shape, lambda *_: (0,) * nd, pipeline_mode=pl.Buffered(1))


def _split3(x):
    hi = x.astype(BF16).astype(F32)
    r = x - hi
    mid = r.astype(BF16).astype(F32)
    lo = (r - mid).astype(BF16).astype(F32)
    return hi, mid, lo


def _mod_kernel(c_ref, w_ref, b_ref, o_ref):
    c = c_ref[...]
    s = (c * jax.nn.sigmoid(c)).astype(BF16)
    o_ref[...] = jnp.dot(s, w_ref[...].astype(BF16), preferred_element_type=F32) + b_ref[...]


def _modulation(c, w, b):
    n = w.shape[1]
    return pl.pallas_call(
        _mod_kernel,
        grid=(n // MOD_BN,),
        in_specs=[
            pl.BlockSpec((BATCH, D_MODEL), lambda j: (0, 0)),
            pl.BlockSpec((D_MODEL, MOD_BN), lambda j: (0, j)),
            pl.BlockSpec((1, MOD_BN), lambda j: (0, j)),
        ],
        out_specs=pl.BlockSpec((BATCH, MOD_BN), lambda j: (0, j)),
        out_shape=jax.ShapeDtypeStruct((BATCH, n), F32),
        compiler_params=_cparams(("arbitrary",)),
        name="modulation",
    )(c, w, b.reshape(1, n))


def _s5_kernel(x_ref, mod_ref, g_ref, win_ref, bblk_ref, ar_ref, ai_ref, cblk_ref, d_ref,
               wglu_ref, bglu_ref, wout_ref, o_ref, bre, bim, st_re, st_im):
    tm = S5_T * BATCH

    @pl.when(pl.program_id(0) == 0)
    def _():
        st_re[...] = jnp.zeros_like(st_re)
        st_im[...] = jnp.zeros_like(st_im)

    x3 = x_ref[...].reshape(S5_T, BATCH, D_MODEL)
    mod = mod_ref[...]
    shift = mod[:, :D_MODEL]
    scale = mod[:, D_MODEL:2 * D_MODEL]
    gate = mod[:, 2 * D_MODEL:]
    ms = jnp.mean(x3 * x3, axis=-1, keepdims=True)
    h3 = (x3 * lax.rsqrt(ms + EPS)) * g_ref[...][None]
    h3 = h3 * (1.0 + scale)[None] + shift[None]
    h = h3.reshape(tm, D_MODEL).astype(BF16)
    uz = jnp.dot(h, win_ref[...], preferred_element_type=F32)
    u = uz[:, :D_MODEL]
    z = uz[:, D_MODEL:]
    ub = u.astype(BF16)

    for c in range(N_KC):
        bu = jnp.dot(ub[:, c * S5_KC:(c + 1) * S5_KC], bblk_ref[c], preferred_element_type=F32)
        bre[2 * c] = bu[:, 0:SCAN_W]
        bre[2 * c + 1] = bu[:, SCAN_W:2 * SCAN_W]
        bim[2 * c] = bu[:, KC_STATES:KC_STATES + SCAN_W]
        bim[2 * c + 1] = bu[:, KC_STATES + SCAN_W:]

    def chunk_body(j, _):
        ar = ar_ref[j]
        ai = ai_ref[j]

        def t_body(t, carry):
            sr, si = carry
            rows = pl.ds(pl.multiple_of(t * BATCH, BATCH), BATCH)
            nr = ar * sr - ai * si + bre[j, rows, :]
            ni = ar * si + ai * sr + bim[j, rows, :]
            bre[j, rows, :] = nr
            bim[j, rows, :] = ni
            return nr, ni

        sr, si = lax.fori_loop(0, S5_T, t_body, (st_re[j], st_im[j]), unroll=4)
        st_re[j] = sr
        st_im[j] = si
        return 0

    lax.fori_loop(0, N_SCAN, chunk_body, 0)

    ys = []
    for c in range(N_KC):
        acc = jnp.dot(bre[2 * c].astype(BF16), cblk_ref[c, 0:SCAN_W, :], preferred_element_type=F32)
        acc += jnp.dot(bre[2 * c + 1].astype(BF16), cblk_ref[c, SCAN_W:2 * SCAN_W, :],
                       preferred_element_type=F32)
        acc += jnp.dot(bim[2 * c].astype(BF16), cblk_ref[c, KC_STATES:KC_STATES + SCAN_W, :],
                       preferred_element_type=F32)
        acc += jnp.dot(bim[2 * c + 1].astype(BF16), cblk_ref[c, KC_STATES + SCAN_W:, :],
                       preferred_element_type=F32)
        ys.append(acc)
    y = jnp.concatenate(ys, axis=1) + d_ref[...] * u
    y = jax.nn.gelu(y)
    gl = jnp.dot(y.astype(BF16), wglu_ref[...], preferred_element_type=F32) + bglu_ref[...]
    y = y * jax.nn.sigmoid(gl)
    y = y * (z * jax.nn.sigmoid(z))
    o = jnp.dot(y.astype(BF16), wout_ref[...], preferred_element_type=F32)
    out3 = x3 + gate[None] * o.reshape(S5_T, BATCH, D_MODEL)
    o_ref[...] = out3.reshape(tm, D_MODEL)


def _s5_layer(xt, mod, g, w_in, bblk, ar, ai, cblk, dvec, w_glu, b_glu, w_out):
    tm = S5_T * BATCH
    return pl.pallas_call(
        _s5_kernel,
        grid=(SEQ // S5_T,),
        in_specs=[
            pl.BlockSpec((tm, D_MODEL), lambda i: (i, 0)),
            _const_spec((BATCH, 3 * D_MODEL)),
            _const_spec((1, D_MODEL)),
            _const_spec((D_MODEL, 2 * D_MODEL)),
            _const_spec((N_KC, S5_KC, 2 * KC_STATES)),
            _const_spec((N_SCAN, BATCH, SCAN_W)),
            _const_spec((N_SCAN, BATCH, SCAN_W)),
            _const_spec((N_KC, 2 * KC_STATES, S5_KC)),
            _const_spec((1, D_MODEL)),
            _const_spec((D_MODEL, D_MODEL)),
            _const_spec((1, D_MODEL)),
            _const_spec((D_MODEL, D_MODEL)),
        ],
        out_specs=pl.BlockSpec((tm, D_MODEL), lambda i: (i, 0)),
        out_shape=jax.ShapeDtypeStruct((SEQ * BATCH, D_MODEL), F32),
        scratch_shapes=[
            pltpu.VMEM((N_SCAN, tm, SCAN_W), F32),
            pltpu.VMEM((N_SCAN, tm, SCAN_W), F32),
            pltpu.VMEM((N_SCAN, BATCH, SCAN_W), F32),
            pltpu.VMEM((N_SCAN, BATCH, SCAN_W), F32),
        ],
        compiler_params=_cparams(("arbitrary",)),
        name="s5_layer",
    )(xt, mod, g, w_in, bblk, ar, ai, cblk, dvec, w_glu, b_glu, w_out)


def _s5_params(log_dt, a_re, a_im, b_re, b_im, c_re, c_im):
    dt = jnp.exp(log_dt)[:, None]
    mag = jnp.exp(a_re * dt)
    abar_r, abar_i = mag * jnp.cos(a_im * dt), mag * jnp.sin(a_im * dt)
    den = a_re * a_re + a_im * a_im
    nr = abar_r - 1.0
    coef_r = (nr * a_re + abar_i * a_im) / den
    coef_i = (abar_i * a_re - nr * a_im) / den
    bb_r = coef_r[..., None] * b_re - coef_i[..., None] * b_im
    bb_i = coef_r[..., None] * b_im + coef_i[..., None] * b_re
    gl = S5_KC // GROUP
    eye = jnp.eye(gl, dtype=F32)

    def pack_b(bb):
        t = bb.transpose(0, 2, 1).reshape(N_KC, gl, GROUP, STATE)
        return jnp.einsum('cgkp,gh->cgkhp', t, eye).reshape(N_KC, S5_KC, KC_STATES)

    def pack_c(cc):
        t = cc.transpose(0, 2, 1).reshape(N_KC, gl, STATE, GROUP)
        return jnp.einsum('cgpk,gh->cgphk', t, eye).reshape(N_KC, KC_STATES, S5_KC)

    bblk = jnp.concatenate([pack_b(bb_r), pack_b(bb_i)], axis=2).astype(BF16)
    cblk = jnp.concatenate([pack_c(c_re), pack_c(-c_im)], axis=1).astype(BF16)

    def rows(a):
        return jnp.broadcast_to(a.reshape(N_SCAN, 1, SCAN_W), (N_SCAN, BATCH, SCAN_W))

    return bblk, cblk, rows(abar_r), rows(abar_i)


def _log_sigmoid(x):
    return jnp.minimum(x, 0.0) - jnp.log1p(jnp.exp(-jnp.abs(x)))


def _proj_kernel(x_ref, mkv_ref, mb_ref, gkv_ref, gb_ref, wk_ref, wvt_ref, wfh_ref, wfl_ref, fb_ref,
                 wq_ref, wzt_ref, kng_ref, qng_ref,
                 kaug_ref, qaug_ref, vt_ref, szt_ref, ft_ref, carry_ref):
    tm = PROJ_TM

    @pl.when(pl.program_id(1) == 0)
    def _():
        carry_ref[...] = jnp.zeros_like(carry_ref)

    x = x_ref[...]
    xn = x * lax.rsqrt(jnp.mean(x * x, axis=-1, keepdims=True) + EPS)
    mkv = mkv_ref[...]
    h2 = (xn * gkv_ref[...]) * (1.0 + mkv[:, D_MODEL:]) + mkv[:, :D_MODEL]
    mb = mb_ref[...]
    h3 = (xn * gb_ref[...]) * (1.0 + mb[:, D_MODEL:2 * D_MODEL]) + mb[:, :D_MODEL]
    h2b = h2.astype(BF16)
    h3b = h3.astype(BF16)
    trans_b = (((1,), (1,)), ((), ()))
    k = jnp.dot(h2b, wk_ref[...], preferred_element_type=F32)
    vt = lax.dot_general(wvt_ref[...], h2b, trans_b, preferred_element_type=F32)
    q = jnp.dot(h3b, wq_ref[...], preferred_element_type=F32)
    zt = lax.dot_general(wzt_ref[...], h3b, trans_b, preferred_element_type=F32)

    h2l = (h2 - h2b.astype(F32)).astype(BF16)
    f = (jnp.dot(h2b, wfh_ref[...], preferred_element_type=F32)
         + jnp.dot(h2b, wfl_ref[...], preferred_element_type=F32)
         + jnp.dot(h2l, wfh_ref[...], preferred_element_type=F32)) + fb_ref[...]
    ls = _log_sigmoid(f)
    ri = lax.broadcasted_iota(jnp.int32, (tm, tm), 0)
    ci = lax.broadcasted_iota(jnp.int32, (tm, tm), 1)
    tri = jnp.where(ci <= ri, 1.0, 0.0).astype(BF16)
    l_hi, l_mid, l_lo = _split3(ls)
    fcum = (jnp.dot(tri, l_hi.astype(BF16), preferred_element_type=F32)
            + jnp.dot(tri, l_mid.astype(BF16), preferred_element_type=F32)
            + jnp.dot(tri, l_lo.astype(BF16), preferred_element_type=F32)) + carry_ref[...]
    carry_ref[...] = fcum[tm - 1:tm, :]

    fct = fcum.T
    for h in range(N_HEADS):
        ft_ref[h] = fct[h:h + 1, :]
    n_hi, n_mid, n_lo = _split3(-fcum)

    lane = lax.broadcasted_iota(jnp.int32, (tm, LANES), 1)
    low = lane < HEAD_DIM
    ones_cols = jnp.where(lane < BIAS_LANE + 3, 1.0, 0.0)
    kng = kng_ref[...]
    qng = qng_ref[...]
    for hp in range(N_HEADS // 2):
        kp = k[:, hp * LANES:(hp + 1) * LANES]
        qp = q[:, hp * LANES:(hp + 1) * LANES]
        ksq = kp * kp
        qsq = qp * qp
        for half in range(2):
            h = 2 * hp + half
            if half == 0:
                ka, qa = kp, qp
                ssk = jnp.sum(jnp.where(low, ksq, 0.0), axis=-1, keepdims=True)
                ssq = jnp.sum(jnp.where(low, qsq, 0.0), axis=-1, keepdims=True)
            else:
                ka, qa = pltpu.roll(kp, HEAD_DIM, 1), pltpu.roll(qp, HEAD_DIM, 1)
                ssk = jnp.sum(jnp.where(low, 0.0, ksq), axis=-1, keepdims=True)
                ssq = jnp.sum(jnp.where(low, 0.0, qsq), axis=-1, keepdims=True)
            kn = (ka * lax.rsqrt(ssk * (1.0 / HEAD_DIM) + EPS)) * kng
            qn = ((qa * lax.rsqrt(ssq * (1.0 / HEAD_DIM) + EPS)) * qng) * (HEAD_DIM ** -0.5)
            bias = jnp.where(lane == BIAS_LANE, n_hi[:, h:h + 1],
                             jnp.where(lane == BIAS_LANE + 1, n_mid[:, h:h + 1],
                                       jnp.where(lane == BIAS_LANE + 2, n_lo[:, h:h + 1], 0.0)))
            kaug_ref[h] = jnp.where(low, kn, bias).astype(BF16)
            qaug_ref[h] = jnp.where(low, qn, ones_cols).astype(BF16)

    vt_ref[...] = vt.reshape(N_HEADS, HEAD_DIM, tm).astype(BF16)
    szt_ref[...] = (zt * jax.nn.sigmoid(zt)).astype(BF16)


def _fox_proj(x1, mkv, mb, gkv, gb, wk, wvt, wfh, wfl, fb, wq, wzt, kng, qng):
    tm = PROJ_TM
    nt = SEQ // tm
    row = lambda b, t: (b, t, 0)
    per_b = lambda b, t: (b, 0, 0)
    return pl.pallas_call(
        _proj_kernel,
        grid=(BATCH, nt),
        in_specs=[
            pl.BlockSpec((None, tm, D_MODEL), row),
            pl.BlockSpec((None, 1, 2 * D_MODEL), per_b),
            pl.BlockSpec((None, 1, 3 * D_MODEL), per_b),
            _const_spec((1, D_MODEL)),
            _const_spec((1, D_MODEL)),
            _const_spec((D_MODEL, D_MODEL)),
            _const_spec((D_MODEL, D_MODEL)),
            _const_spec((D_MODEL, LANES)),
            _const_spec((D_MODEL, LANES)),
            _const_spec((1, LANES)),
            _const_spec((D_MODEL, D_MODEL)),
            _const_spec((D_MODEL, D_MODEL)),
            _const_spec((1, LANES)),
            _const_spec((1, LANES)),
        ],
        out_specs=[
            pl.BlockSpec((None, N_HEADS, tm, LANES), lambda b, t: (b, 0, t, 0)),
            pl.BlockSpec((None, N_HEADS, tm, LANES), lambda b, t: (b, 0, t, 0)),
            pl.BlockSpec((None, N_HEADS, None, HEAD_DIM, tm), lambda b, t: (b, 0, t, 0, 0)),
            pl.BlockSpec((None, D_MODEL, tm), lambda b, t: (b, 0, t)),
            pl.BlockSpec((None, N_HEADS, 1, tm), lambda b, t: (b, 0, 0, t)),
        ],
        out_shape=[
            jax.ShapeDtypeStruct((BATCH, N_HEADS, SEQ, LANES), BF16),
            jax.ShapeDtypeStruct((BATCH, N_HEADS, SEQ, LANES), BF16),
            jax.ShapeDtypeStruct((BATCH, N_HEADS, nt, HEAD_DIM, tm), BF16),
            jax.ShapeDtypeStruct((BATCH, D_MODEL, SEQ), BF16),
            jax.ShapeDtypeStruct((BATCH, N_HEADS, 1, SEQ), F32),
        ],
        scratch_shapes=[pltpu.VMEM((1, LANES), F32)],
        compiler_params=_cparams(("arbitrary", "arbitrary")),
        name="fox_proj",
    )(x1, mkv, mb, gkv, gb, wk, wvt, wfh, wfl, fb, wq, wzt, kng, qng)


def _attn_kernel(q_ref, k_ref, v_ref, fq_ref, o_ref):
    t = ATT_T
    qi = pl.program_id(2)
    q = q_ref[...]
    fq = fq_ref[...]
    trans_b = (((1,), (1,)), ((), ()))

    def block(kj, carry, diagonal):
        m, l, acc = carry
        k = k_ref[pl.ds(pl.multiple_of(kj * t, t), t), :]
        s = lax.dot_general(k, q, trans_b, preferred_element_type=F32)
        if diagonal:
            ki = lax.broadcasted_iota(jnp.int32, (t, t), 0)
            qq = lax.broadcasted_iota(jnp.int32, (t, t), 1)
            s = jnp.where(ki <= qq, s, -jnp.inf)
        m_new = jnp.maximum(m, jnp.max(s, axis=0, keepdims=True) + fq)
        alpha = jnp.exp(m - m_new)
        p = jnp.exp(s + (fq - m_new))
        l = alpha * l + jnp.sum(p, axis=0, keepdims=True)
        acc = alpha * acc + jnp.dot(v_ref[kj], p.astype(BF16), preferred_element_type=F32)
        return m_new, l, acc

    init = (jnp.full((1, t), -jnp.inf, F32), jnp.zeros((1, t), F32), jnp.zeros((HEAD_DIM, t), F32))
    carry = lax.fori_loop(0, qi, lambda kj, c: block(kj, c, False), init)
    _, l, acc = block(qi, carry, True)
    o_ref[...] = (acc * (1.0 / l)).astype(BF16)


def _fox_attn(qaug, kaug, vt, ft):
    t = ATT_T
    return pl.pallas_call(
        _attn_kernel,
        grid=(BATCH, N_HEADS, SEQ // t),
        in_specs=[
            pl.BlockSpec((None, None, t, LANES), lambda b, h, i: (b, h, i, 0)),
            pl.BlockSpec((None, None, SEQ, LANES), lambda b, h, i: (b, h, 0, 0)),
            pl.BlockSpec((None, None, N_KT, HEAD_DIM, t), lambda b, h, i: (b, h, 0, 0, 0)),
            pl.BlockSpec((None, None, 1, t), lambda b, h, i: (b, h, 0, i)),
        ],
        out_specs=pl.BlockSpec((None, None, HEAD_DIM, t), lambda b, h, i: (b, h, 0, i)),
        out_shape=jax.ShapeDtypeStruct((BATCH, N_HEADS, HEAD_DIM, SEQ), BF16),
        compiler_params=_cparams(("arbitrary", "arbitrary", "arbitrary")),
        name="fox_attn",
    )(qaug, kaug, vt, ft)


def _out_kernel(ot_ref, szt_ref, x_ref, mb_ref, w_ref, o_ref):
    tm = PROJ_TM
    yt = ot_ref[...].astype(F32).reshape(D_MODEL, tm) * szt_ref[...].astype(F32)
    y = yt.T.astype(BF16)
    out = jnp.dot(y, w_ref[...], preferred_element_type=F32)
    gate = mb_ref[...][:, 2 * D_MODEL:]
    o_ref[...] = x_ref[...] + gate * out


def _fox_out(ot, szt, x1, mb, w_out):
    tm = PROJ_TM
    return pl.pallas_call(
        _out_kernel,
        grid=(BATCH, SEQ // tm),
        in_specs=[
            pl.BlockSpec((None, N_HEADS, HEAD_DIM, tm), lambda b, t: (b, 0, 0, t)),
            pl.BlockSpec((None, D_MODEL, tm), lambda b, t: (b, 0, t)),
            pl.BlockSpec((None, tm, D_MODEL), lambda b, t: (b, t, 0)),
            pl.BlockSpec((None, 1, 3 * D_MODEL), lambda b, t: (b, 0, 0)),
            _const_spec((D_MODEL, D_MODEL)),
        ],
        out_specs=pl.BlockSpec((None, tm, D_MODEL), lambda b, t: (b, t, 0)),
        out_shape=jax.ShapeDtypeStruct((BATCH, SEQ, D_MODEL), F32),
        compiler_params=_cparams(("arbitrary", "arbitrary")),
        name="fox_out",
    )(ot, szt, x1, mb, w_out)


def kernel(x, c, a_norm_g, a_mod_w, a_mod_b, a_w_in, a_log_dt, a_A_re, a_A_im, a_B_re, a_B_im,
           a_C_re, a_C_im, a_D, a_w_glu, a_b_glu, a_w_out, kv_norm_g, kv_mod_w, kv_mod_b, kv_w,
           kv_f_bias, k_norm_g, b_norm_g, b_mod_w, b_mod_b, b_w_in, q_norm_g, b_w_out):
    assert x.shape == (BATCH, SEQ, D_MODEL) and a_mod_w.shape[0] == 1 and b_mod_w.shape[0] == 1
    aw = N_HEADS * HEAD_DIM

    mod_a = _modulation(c, a_mod_w[0], a_mod_b[0])
    mod_kv = _modulation(c, kv_mod_w, kv_mod_b)
    mod_b = _modulation(c, b_mod_w[0], b_mod_b[0])

    bblk, cblk, ar, ai = _s5_params(a_log_dt[0], a_A_re[0], a_A_im[0], a_B_re[0], a_B_im[0],
                                    a_C_re[0], a_C_im[0])
    xt = x.transpose(1, 0, 2).reshape(SEQ * BATCH, D_MODEL)
    x1t = _s5_layer(xt, mod_a, a_norm_g[0].reshape(1, D_MODEL), a_w_in[0].astype(BF16), bblk, ar, ai,
                    cblk, a_D[0].reshape(1, D_MODEL), a_w_glu[0].astype(BF16),
                    a_b_glu[0].reshape(1, D_MODEL), a_w_out[0].astype(BF16))
    x1 = x1t.reshape(SEQ, BATCH, D_MODEL).transpose(1, 0, 2)

    wk = kv_w[:, :aw].astype(BF16)
    wvt = kv_w[:, aw:2 * aw].T.astype(BF16)
    wf = jnp.pad(kv_w[:, 2 * aw:], ((0, 0), (0, LANES - N_HEADS)))
    wfh = wf.astype(BF16)
    wfl = (wf - wfh.astype(F32)).astype(BF16)
    fb = jnp.pad(kv_f_bias, (0, LANES - N_HEADS)).reshape(1, LANES)
    wq = b_w_in[0][:, :aw].astype(BF16)
    wzt = b_w_in[0][:, aw:].T.astype(BF16)
    kng = jnp.tile(k_norm_g, 2).reshape(1, LANES)
    qng = jnp.tile(q_norm_g[0], 2).reshape(1, LANES)
    mkv3 = mod_kv.reshape(BATCH, 1, 2 * D_MODEL)
    mb3 = mod_b.reshape(BATCH, 1, 3 * D_MODEL)
    kaug, qaug, vt, szt, ft = _fox_proj(x1, mkv3, mb3, kv_norm_g.reshape(1, D_MODEL),
                                        b_norm_g[0].reshape(1, D_MODEL), wk, wvt, wfh, wfl, fb,
                                        wq, wzt, kng, qng)
    ot = _fox_attn(qaug, kaug, vt, ft)
    return _fox_out(ot, szt, x1, mb3, b_w_out[0].astype(BF16))
```

```python
import functools

import jax
import jax.numpy as jnp
from jax import lax
from jax.experimental import pallas as pl
from jax.experimental.pallas import tpu as pltpu

D_MODEL = 1024
BATCH = 8
SEQ = 2048
GROUP = 16
N_GROUPS = D_MODEL // GROUP
STATE = 64
N_STATES = N_GROUPS * STATE
N_HEADS = 16
HEAD_DIM = 64
EPS = 1e-6

F32 = jnp.float32
BF16 = jnp.bfloat16

SUBLANES = 8
LANES = 128
MXU_DIM = 256
VMEM_LIMIT_BYTES = 56 * 1024 * 1024

S5_T = 32
SCAN_W = 512
N_SCAN = N_STATES // SCAN_W
S5_KC = MXU_DIM
N_KC = D_MODEL // S5_KC
KC_STATES = (S5_KC // GROUP) * STATE
PROJ_TM = 512
ATT_T = 256
N_KT = SEQ // ATT_T
KT_PER_PROJ = PROJ_TM // ATT_T
HEAD_GROUP = 16
V_ROWS = HEAD_DIM + 16
BIAS_LANE = HEAD_DIM
MOD_BN = 512
LOG2E = 1.4426950408889634


def _cparams(sem):
    return pltpu.CompilerParams(dimension_semantics=sem, vmem_limit_bytes=VMEM_LIMIT_BYTES)


def _const_spec(shape):
    nd = len(shape)
    return pl.BlockSpec(shape, lambda *_: (0,) * nd, pipeline_mode=pl.Buffered(1))


def _split3(x):
    hi = x.astype(BF16).astype(F32)
    r = x - hi
    mid = r.astype(BF16).astype(F32)
    lo = (r - mid).astype(BF16).astype(F32)
    return hi, mid, lo


def _mod_kernel(c_ref, w_ref, b_ref, o_ref):
    c = c_ref[...]
    s = (c * jax.nn.sigmoid(c)).astype(BF16)
    o_ref[...] = jnp.dot(s, w_ref[...].astype(BF16), preferred_element_type=F32) + b_ref[...]


def _modulation(c, w, b):
    n = w.shape[1]
    return pl.pallas_call(
        _mod_kernel,
        grid=(n // MOD_BN,),
        in_specs=[
            pl.BlockSpec((BATCH, D_MODEL), lambda j: (0, 0)),
            pl.BlockSpec((D_MODEL, MOD_BN), lambda j: (0, j)),
            pl.BlockSpec((1, MOD_BN), lambda j: (0, j)),
        ],
        out_specs=pl.BlockSpec((BATCH, MOD_BN), lambda j: (0, j)),
        out_shape=jax.ShapeDtypeStruct((BATCH, n), F32),
        compiler_params=_cparams(("arbitrary",)),
        name="modulation",
    )(c, w, b.reshape(1, n))


def _s5_kernel(x_ref, mod_ref, g_ref, win_ref, bblk_ref, ar_ref, ai_ref, cblk_ref, d_ref,
               wglu_ref, bglu_ref, wout_ref, o_ref, bre, bim, st_re, st_im):
    tm = S5_T * BATCH

    @pl.when(pl.program_id(0) == 0)
    def _():
        st_re[...] = jnp.zeros_like(st_re)
        st_im[...] = jnp.zeros_like(st_im)

    x3 = x_ref[...].reshape(S5_T, BATCH, D_MODEL)
    mod = mod_ref[...]
    shift = mod[:, :D_MODEL]
    scale = mod[:, D_MODEL:2 * D_MODEL]
    gate = mod[:, 2 * D_MODEL:]
    ms = jnp.mean(x3 * x3, axis=-1, keepdims=True)
    h3 = (x3 * lax.rsqrt(ms + EPS)) * g_ref[...][None]
    h3 = h3 * (1.0 + scale)[None] + shift[None]
    h = h3.reshape(tm, D_MODEL).astype(BF16)
    uz = jnp.dot(h, win_ref[...], preferred_element_type=F32)
    u = uz[:, :D_MODEL]
    z = uz[:, D_MODEL:]
    ub = u.astype(BF16)

    for c in range(N_KC):
        bu = jnp.dot(ub[:, c * S5_KC:(c + 1) * S5_KC], bblk_ref[c], preferred_element_type=F32)
        bre[2 * c] = bu[:, 0:SCAN_W]
        bre[2 * c + 1] = bu[:, SCAN_W:2 * SCAN_W]
        bim[2 * c] = bu[:, KC_STATES:KC_STATES + SCAN_W]
        bim[2 * c + 1] = bu[:, KC_STATES + SCAN_W:]

    def chunk_body(j, _):
        ar = ar_ref[j]
        ai = ai_ref[j]

        def t_body(t, carry):
            sr, si = carry
            rows = pl.ds(pl.multiple_of(t * BATCH, BATCH), BATCH)
            nr = ar * sr - ai * si + bre[j, rows, :]
            ni = ar * si + ai * sr + bim[j, rows, :]
            bre[j, rows, :] = nr
            bim[j, rows, :] = ni
            return nr, ni

        sr, si = lax.fori_loop(0, S5_T, t_body, (st_re[j], st_im[j]), unroll=4)
        st_re[j] = sr
        st_im[j] = si
        return 0

    lax.fori_loop(0, N_SCAN, chunk_body, 0)

    ys = []
    for c in range(N_KC):
        acc = jnp.dot(bre[2 * c].astype(BF16), cblk_ref[c, 0:SCAN_W, :], preferred_element_type=F32)
        acc += jnp.dot(bre[2 * c + 1].astype(BF16), cblk_ref[c, SCAN_W:2 * SCAN_W, :],
                       preferred_element_type=F32)
        acc += jnp.dot(bim[2 * c].astype(BF16), cblk_ref[c, KC_STATES:KC_STATES + SCAN_W, :],
                       preferred_element_type=F32)
        acc += jnp.dot(bim[2 * c + 1].astype(BF16), cblk_ref[c, KC_STATES + SCAN_W:, :],
                       preferred_element_type=F32)
        ys.append(acc)
    y = jnp.concatenate(ys, axis=1) + d_ref[...] * u
    y = jax.nn.gelu(y)
    gl = jnp.dot(y.astype(BF16), wglu_ref[...], preferred_element_type=F32) + bglu_ref[...]
    y = y * jax.nn.sigmoid(gl)
    y = y * (z * jax.nn.sigmoid(z))
    o = jnp.dot(y.astype(BF16), wout_ref[...], preferred_element_type=F32)
    out3 = x3 + gate[None] * o.reshape(S5_T, BATCH, D_MODEL)
    o_ref[...] = out3.reshape(tm, D_MODEL)


def _s5_layer(xt, mod, g, w_in, bblk, ar, ai, cblk, dvec, w_glu, b_glu, w_out):
    tm = S5_T * BATCH
    return pl.pallas_call(
        _s5_kernel,
        grid=(SEQ // S5_T,),
        in_specs=[
            pl.BlockSpec((tm, D_MODEL), lambda i: (i, 0)),
            _const_spec((BATCH, 3 * D_MODEL)),
            _const_spec((1, D_MODEL)),
            _const_spec((D_MODEL, 2 * D_MODEL)),
            _const_spec((N_KC, S5_KC, 2 * KC_STATES)),
            _const_spec((N_SCAN, BATCH, SCAN_W)),
            _const_spec((N_SCAN, BATCH, SCAN_W)),
            _const_spec((N_KC, 2 * KC_STATES, S5_KC)),
            _const_spec((1, D_MODEL)),
            _const_spec((D_MODEL, D_MODEL)),
            _const_spec((1, D_MODEL)),
            _const_spec((D_MODEL, D_MODEL)),
        ],
        out_specs=pl.BlockSpec((tm, D_MODEL), lambda i: (i, 0)),
        out_shape=jax.ShapeDtypeStruct((SEQ * BATCH, D_MODEL), F32),
        scratch_shapes=[
            pltpu.VMEM((N_SCAN, tm, SCAN_W), F32),
            pltpu.VMEM((N_SCAN, tm, SCAN_W), F32),
            pltpu.VMEM((N_SCAN, BATCH, SCAN_W), F32),
            pltpu.VMEM((N_SCAN, BATCH, SCAN_W), F32),
        ],
        compiler_params=_cparams(("arbitrary",)),
        name="s5_layer",
    )(xt, mod, g, w_in, bblk, ar, ai, cblk, dvec, w_glu, b_glu, w_out)


def _s5_params(log_dt, a_re, a_im, b_re, b_im, c_re, c_im):
    dt = jnp.exp(log_dt)[:, None]
    mag = jnp.exp(a_re * dt)
    abar_r, abar_i = mag * jnp.cos(a_im * dt), mag * jnp.sin(a_im * dt)
    den = a_re * a_re + a_im * a_im
    nr = abar_r - 1.0
    coef_r = (nr * a_re + abar_i * a_im) / den
    coef_i = (abar_i * a_re - nr * a_im) / den
    bb_r = coef_r[..., None] * b_re - coef_i[..., None] * b_im
    bb_i = coef_r[..., None] * b_im + coef_i[..., None] * b_re
    gl = S5_KC // GROUP
    eye = jnp.eye(gl, dtype=F32)

    def pack_b(bb):
        t = bb.transpose(0, 2, 1).reshape(N_KC, gl, GROUP, STATE)
        return jnp.einsum('cgkp,gh->cgkhp', t, eye).reshape(N_KC, S5_KC, KC_STATES)

    def pack_c(cc):
        t = cc.transpose(0, 2, 1).reshape(N_KC, gl, STATE, GROUP)
        return jnp.einsum('cgpk,gh->cgphk', t, eye).reshape(N_KC, KC_STATES, S5_KC)

    bblk = jnp.concatenate([pack_b(bb_r), pack_b(bb_i)], axis=2).astype(BF16)
    cblk = jnp.concatenate([pack_c(c_re), pack_c(-c_im)], axis=1).astype(BF16)

    def rows(a):
        return jnp.broadcast_to(a.reshape(N_SCAN, 1, SCAN_W), (N_SCAN, BATCH, SCAN_W))

    return bblk, cblk, rows(abar_r), rows(abar_i)


def _log_sigmoid(x):
    return jnp.minimum(x, 0.0) - jnp.log1p(jnp.exp(-jnp.abs(x)))


def _proj_kernel(x_ref, mkv_ref, mb_ref, gkv_ref, gb_ref, wk_ref, wvt_ref, wfh_ref, wfl_ref, fb_ref,
                 wq_ref, wzt_ref, kng_ref, qng_ref,
                 kaug_ref, qaug_ref, vt_ref, szt_ref, ft_ref, carry_ref):
    tm = PROJ_TM

    @pl.when(pl.program_id(1) == 0)
    def _():
        carry_ref[...] = jnp.zeros_like(carry_ref)

    x = x_ref[...]
    xn = x * lax.rsqrt(jnp.mean(x * x, axis=-1, keepdims=True) + EPS)
    mkv = mkv_ref[...]
    h2 = (xn * gkv_ref[...]) * (1.0 + mkv[:, D_MODEL:]) + mkv[:, :D_MODEL]
    mb = mb_ref[...]
    h3 = (xn * gb_ref[...]) * (1.0 + mb[:, D_MODEL:2 * D_MODEL]) + mb[:, :D_MODEL]
    h2b = h2.astype(BF16)
    h3b = h3.astype(BF16)
    trans_b = (((1,), (1,)), ((), ()))
    k = jnp.dot(h2b, wk_ref[...], preferred_element_type=F32)
    vt = lax.dot_general(wvt_ref[...], h2b, trans_b, preferred_element_type=F32)
    q = jnp.dot(h3b, wq_ref[...], preferred_element_type=F32)
    zt = lax.dot_general(wzt_ref[...], h3b, trans_b, preferred_element_type=F32)

    h2l = (h2 - h2b.astype(F32)).astype(BF16)
    f = (jnp.dot(h2b, wfh_ref[...], preferred_element_type=F32)
         + jnp.dot(h2b, wfl_ref[...], preferred_element_type=F32)
         + jnp.dot(h2l, wfh_ref[...], preferred_element_type=F32)) + fb_ref[...]
    ls = _log_sigmoid(f)
    ri = lax.broadcasted_iota(jnp.int32, (tm, tm), 0)
    ci = lax.broadcasted_iota(jnp.int32, (tm, tm), 1)
    tri = jnp.where(ci <= ri, 1.0, 0.0).astype(BF16)
    l_hi, l_mid, l_lo = _split3(ls)
    fcum = (jnp.dot(tri, l_hi.astype(BF16), preferred_element_type=F32)
            + jnp.dot(tri, l_mid.astype(BF16), preferred_element_type=F32)
            + jnp.dot(tri, l_lo.astype(BF16), preferred_element_type=F32)) + carry_ref[...]
    carry_ref[...] = fcum[tm - 1:tm, :]

    f2 = fcum * LOG2E
    fct = f2.T
    for h in range(N_HEADS):
        ft_ref[h] = fct[h:h + 1, :]
    n_hi, n_mid, n_lo = _split3(-f2)

    lane = lax.broadcasted_iota(jnp.int32, (tm, LANES), 1)
    low = lane < HEAD_DIM
    ones_cols = jnp.where(lane < BIAS_LANE + 3, 1.0, 0.0)
    kng = kng_ref[...]
    qng = qng_ref[...]
    for hp in range(N_HEADS // 2):
        kp = k[:, hp * LANES:(hp + 1) * LANES]
        qp = q[:, hp * LANES:(hp + 1) * LANES]
        ksq = kp * kp
        qsq = qp * qp
        for half in range(2):
            h = 2 * hp + half
            if half == 0:
                ka, qa = kp, qp
                ssk = jnp.sum(jnp.where(low, ksq, 0.0), axis=-1, keepdims=True)
                ssq = jnp.sum(jnp.where(low, qsq, 0.0), axis=-1, keepdims=True)
            else:
                ka, qa = pltpu.roll(kp, HEAD_DIM, 1), pltpu.roll(qp, HEAD_DIM, 1)
                ssk = jnp.sum(jnp.where(low, 0.0, ksq), axis=-1, keepdims=True)
                ssq = jnp.sum(jnp.where(low, 0.0, qsq), axis=-1, keepdims=True)
            kn = (ka * lax.rsqrt(ssk * (1.0 / HEAD_DIM) + EPS)) * kng
            qn = ((qa * lax.rsqrt(ssq * (1.0 / HEAD_DIM) + EPS)) * qng) * (HEAD_DIM ** -0.5 * LOG2E)
            bias = jnp.where(lane == BIAS_LANE, n_hi[:, h:h + 1],
                             jnp.where(lane == BIAS_LANE + 1, n_mid[:, h:h + 1],
                                       jnp.where(lane == BIAS_LANE + 2, n_lo[:, h:h + 1], 0.0)))
            kaug_ref[h] = jnp.where(low, kn, bias).astype(BF16)
            qaug_ref[h] = jnp.where(low, qn, ones_cols).astype(BF16)

    vt3 = vt.reshape(N_HEADS, HEAD_DIM, tm).astype(BF16)
    pad_row = lax.broadcasted_iota(jnp.int32, (N_HEADS, V_ROWS - HEAD_DIM, ATT_T), 1)
    ones_rows = jnp.where(pad_row == 0, 1.0, 0.0).astype(BF16)
    for kt in range(KT_PER_PROJ):
        vt_ref[:, kt, 0:HEAD_DIM, :] = vt3[:, :, kt * ATT_T:(kt + 1) * ATT_T]
        vt_ref[:, kt, HEAD_DIM:, :] = ones_rows
    szt_ref[...] = (zt * jax.nn.sigmoid(zt)).astype(BF16)


def _fox_proj(x1, mkv, mb, gkv, gb, wk, wvt, wfh, wfl, fb, wq, wzt, kng, qng):
    tm = PROJ_TM
    nt = SEQ // tm
    row = lambda b, t: (b, t, 0)
    per_b = lambda b, t: (b, 0, 0)
    return pl.pallas_call(
        _proj_kernel,
        grid=(BATCH, nt),
        in_specs=[
            pl.BlockSpec((None, tm, D_MODEL), row),
            pl.BlockSpec((None, 1, 2 * D_MODEL), per_b),
            pl.BlockSpec((None, 1, 3 * D_MODEL), per_b),
            _const_spec((1, D_MODEL)),
            _const_spec((1, D_MODEL)),
            _const_spec((D_MODEL, D_MODEL)),
            _const_spec((D_MODEL, D_MODEL)),
            _const_spec((D_MODEL, LANES)),
            _const_spec((D_MODEL, LANES)),
            _const_spec((1, LANES)),
            _const_spec((D_MODEL, D_MODEL)),
            _const_spec((D_MODEL, D_MODEL)),
            _const_spec((1, LANES)),
            _const_spec((1, LANES)),
        ],
        out_specs=[
            pl.BlockSpec((None, N_HEADS, tm, LANES), lambda b, t: (b, 0, t, 0)),
            pl.BlockSpec((None, N_HEADS, tm, LANES), lambda b, t: (b, 0, t, 0)),
            pl.BlockSpec((None, N_HEADS, KT_PER_PROJ, V_ROWS, ATT_T), lambda b, t: (b, 0, t, 0, 0)),
            pl.BlockSpec((None, D_MODEL, tm), lambda b, t: (b, 0, t)),
            pl.BlockSpec((None, N_HEADS, 1, tm), lambda b, t: (b, 0, 0, t)),
        ],
        out_shape=[
            jax.ShapeDtypeStruct((BATCH, N_HEADS, SEQ, LANES), BF16),
            jax.ShapeDtypeStruct((BATCH, N_HEADS, SEQ, LANES), BF16),
            jax.ShapeDtypeStruct((BATCH, N_HEADS, N_KT, V_ROWS, ATT_T), BF16),
            jax.ShapeDtypeStruct((BATCH, D_MODEL, SEQ), BF16),
            jax.ShapeDtypeStruct((BATCH, N_HEADS, 1, SEQ), F32),
        ],
        scratch_shapes=[pltpu.VMEM((1, LANES), F32)],
        compiler_params=_cparams(("arbitrary", "arbitrary")),
        name="fox_proj",
    )(x1, mkv, mb, gkv, gb, wk, wvt, wfh, wfl, fb, wq, wzt, kng, qng)


def _attn_kernel(q_ref, k_ref, v_ref, fq_ref, szt_ref, x_ref, mb_ref, w_ref, o_ref,
                 acc_ref, m_ref, ot_ref, s_ref):
    t = ATT_T
    qi = pl.program_id(1)
    trans_b = (((1,), (1,)), ((), ()))
    ki = lax.broadcasted_iota(jnp.int32, (t, t), 0)
    qq = lax.broadcasted_iota(jnp.int32, (t, t), 1)
    visible = ki <= qq

    def scores(i, h, kj, diagonal):
        k = k_ref[h, pl.ds(pl.multiple_of(kj * t, t), t), :]
        s = lax.dot_general(k, q_ref[h], trans_b, preferred_element_type=F32)
        if diagonal:
            s = jnp.where(visible, s, -jnp.inf)
        s_ref[i] = s
        return jnp.max(s, axis=0, keepdims=True)

    def accumulate(i, h, kj, s_max):
        fq = fq_ref[h]
        m_old = m_ref[i]
        m_new = jnp.maximum(m_old, s_max + fq)
        alpha = jnp.exp2(m_old - m_new)
        p = jnp.exp2(s_ref[i] + (fq - m_new)).astype(BF16)
        acc_ref[i] = alpha * acc_ref[i] + jnp.dot(v_ref[h, kj], p, preferred_element_type=F32)
        m_ref[i] = m_new

    def group_body(hg, _):
        heads = [(i, hg * HEAD_GROUP + i) for i in range(HEAD_GROUP)]
        for i, _h in heads:
            m_ref[i] = jnp.full((1, t), -jnp.inf, F32)
            acc_ref[i] = jnp.zeros((V_ROWS, t), F32)

        def blocks(kj, diagonal):
            s_max = [scores(i, h, kj, diagonal) for i, h in heads]
            for (i, h), sm in zip(heads, s_max):
                accumulate(i, h, kj, sm)

        def k_body(kj, _):
            blocks(kj, False)
            return 0

        lax.fori_loop(0, qi, k_body, 0)
        blocks(qi, True)
        for i, h in heads:
            a = acc_ref[i]
            rows = pl.ds(pl.multiple_of(h * HEAD_DIM, HEAD_DIM), HEAD_DIM)
            ot_ref[rows, :] = a[0:HEAD_DIM] * (1.0 / a[HEAD_DIM:HEAD_DIM + 1])
        return 0

    lax.fori_loop(0, N_HEADS // HEAD_GROUP, group_body, 0)

    yt = ot_ref[...] * szt_ref[...].astype(F32)
    y = yt.T.astype(BF16)
    out = jnp.dot(y, w_ref[...], preferred_element_type=F32)
    gate = mb_ref[...][:, 2 * D_MODEL:]
    o_ref[...] = x_ref[...] + gate * out


def _fox_attn(qaug, kaug, vt, ft, szt, x1, mb, w_out):
    t = ATT_T
    return pl.pallas_call(
        _attn_kernel,
        grid=(BATCH, SEQ // t),
        in_specs=[
            pl.BlockSpec((None, N_HEADS, t, LANES), lambda b, i: (b, 0, i, 0)),
            pl.BlockSpec((None, N_HEADS, SEQ, LANES), lambda b, i: (b, 0, 0, 0)),
            pl.BlockSpec((None, N_HEADS, N_KT, V_ROWS, t), lambda b, i: (b, 0, 0, 0, 0)),
            pl.BlockSpec((None, N_HEADS, 1, t), lambda b, i: (b, 0, 0, i)),
            pl.BlockSpec((None, D_MODEL, t), lambda b, i: (b, 0, i)),
            pl.BlockSpec((None, t, D_MODEL), lambda b, i: (b, i, 0)),
            pl.BlockSpec((None, 1, 3 * D_MODEL), lambda b, i: (b, 0, 0)),
            _const_spec((D_MODEL, D_MODEL)),
        ],
        out_specs=pl.BlockSpec((None, t, D_MODEL), lambda b, i: (b, i, 0)),
        out_shape=jax.ShapeDtypeStruct((BATCH, SEQ, D_MODEL), F32),
        scratch_shapes=[
            pltpu.VMEM((HEAD_GROUP, V_ROWS, t), F32),
            pltpu.VMEM((HEAD_GROUP, 1, t), F32),
            pltpu.VMEM((D_MODEL, t), F32),
            pltpu.VMEM((HEAD_GROUP, t, t), F32),
        ],
        compiler_params=_cparams(("arbitrary", "arbitrary")),
        name="fox_attn",
    )(qaug, kaug, vt, ft, szt, x1, mb, w_out)


def kernel(x, c, a_norm_g, a_mod_w, a_mod_b, a_w_in, a_log_dt, a_A_re, a_A_im, a_B_re, a_B_im,
           a_C_re, a_C_im, a_D, a_w_glu, a_b_glu, a_w_out, kv_norm_g, kv_mod_w, kv_mod_b, kv_w,
           kv_f_bias, k_norm_g, b_norm_g, b_mod_w, b_mod_b, b_w_in, q_norm_g, b_w_out):
    assert x.shape == (BATCH, SEQ, D_MODEL) and a_mod_w.shape[0] == 1 and b_mod_w.shape[0] == 1
    aw = N_HEADS * HEAD_DIM

    mod_a = _modulation(c, a_mod_w[0], a_mod_b[0])
    mod_kv = _modulation(c, kv_mod_w, kv_mod_b)
    mod_b = _modulation(c, b_mod_w[0], b_mod_b[0])

    bblk, cblk, ar, ai = _s5_params(a_log_dt[0], a_A_re[0], a_A_im[0], a_B_re[0], a_B_im[0],
                                    a_C_re[0], a_C_im[0])
    xt = x.transpose(1, 0, 2).reshape(SEQ * BATCH, D_MODEL)
    x1t = _s5_layer(xt, mod_a, a_norm_g[0].reshape(1, D_MODEL), a_w_in[0].astype(BF16), bblk, ar, ai,
                    cblk, a_D[0].reshape(1, D_MODEL), a_w_glu[0].astype(BF16),
                    a_b_glu[0].reshape(1, D_MODEL), a_w_out[0].astype(BF16))
    x1 = x1t.reshape(SEQ, BATCH, D_MODEL).transpose(1, 0, 2)

    wk = kv_w[:, :aw].astype(BF16)
    wvt = kv_w[:, aw:2 * aw].T.astype(BF16)
    wf = jnp.pad(kv_w[:, 2 * aw:], ((0, 0), (0, LANES - N_HEADS)))
    wfh = wf.astype(BF16)
    wfl = (wf - wfh.astype(F32)).astype(BF16)
    fb = jnp.pad(kv_f_bias, (0, LANES - N_HEADS)).reshape(1, LANES)
    wq = b_w_in[0][:, :aw].astype(BF16)
    wzt = b_w_in[0][:, aw:].T.astype(BF16)
    kng = jnp.tile(k_norm_g, 2).reshape(1, LANES)
    qng = jnp.tile(q_norm_g[0], 2).reshape(1, LANES)
    mkv3 = mod_kv.reshape(BATCH, 1, 2 * D_MODEL)
    mb3 = mod_b.reshape(BATCH, 1, 3 * D_MODEL)
    kaug, qaug, vt, szt, ft = _fox_proj(x1, mkv3, mb3, kv_norm_g.reshape(1, D_MODEL),
                                        b_norm_g[0].reshape(1, D_MODEL), wk, wvt, wfh, wfl, fb,
                                        wq, wzt, kng, qng)
    return _fox_attn(qaug, kaug, vt, ft, szt, x1, mb3, b_w_out[0].astype(BF16))
```

```python
import functools

import jax
import jax.numpy as jnp
from jax import lax
from jax.experimental import pallas as pl
from jax.experimental.pallas import tpu as pltpu

D_MODEL = 1024
BATCH = 8
SEQ = 2048
GROUP = 16
N_GROUPS = D_MODEL // GROUP
STATE = 64
N_STATES = N_GROUPS * STATE
N_HEADS = 16
HEAD_DIM = 64
EPS = 1e-6

F32 = jnp.float32
BF16 = jnp.bfloat16

SUBLANES = 8
LANES = 128
MXU_DIM = 256
VMEM_LIMIT_BYTES = 56 * 1024 * 1024

S5_T = 32
SCAN_W = 512
N_SCAN = N_STATES // SCAN_W
S5_KC = MXU_DIM
N_KC = D_MODEL // S5_KC
KC_STATES = (S5_KC // GROUP) * STATE
PROJ_TM = 512
ATT_T = 256
N_KT = SEQ // ATT_T
KT_PER_PROJ = PROJ_TM // ATT_T
HEAD_GROUP = 16
V_ROWS = HEAD_DIM + 16
BIAS_LANE = HEAD_DIM
MOD_BN = 512
LOG2E = 1.4426950408889634


def _cparams(sem):
    return pltpu.CompilerParams(dimension_semantics=sem, vmem_limit_bytes=VMEM_LIMIT_BYTES)


def _const_spec(shape):
    nd = len(shape)
    return pl.BlockSpec(shape, lambda *_: (0,) * nd, pipeline_mode=pl.Buffered(1))


def _split3(x):
    hi = x.astype(BF16).astype(F32)
    r = x - hi
    mid = r.astype(BF16).astype(F32)
    lo = (r - mid).astype(BF16).astype(F32)
    return hi, mid, lo


def _mod_kernel(c_ref, w_ref, b_ref, o_ref):
    c = c_ref[...]
    s = (c * jax.nn.sigmoid(c)).astype(BF16)
    o_ref[...] = jnp.dot(s, w_ref[...].astype(BF16), preferred_element_type=F32) + b_ref[...]


def _modulation(c, w, b):
    n = w.shape[1]
    return pl.pallas_call(
        _mod_kernel,
        grid=(n // MOD_BN,),
        in_specs=[
            pl.BlockSpec((BATCH, D_MODEL), lambda j: (0, 0)),
            pl.BlockSpec((D_MODEL, MOD_BN), lambda j: (0, j)),
            pl.BlockSpec((1, MOD_BN), lambda j: (0, j)),
        ],
        out_specs=pl.BlockSpec((BATCH, MOD_BN), lambda j: (0, j)),
        out_shape=jax.ShapeDtypeStruct((BATCH, n), F32),
        compiler_params=_cparams(("arbitrary",)),
        name="modulation",
    )(c, w, b.reshape(1, n))


def _s5_kernel(x_ref, mod_ref, g_ref, win_ref, bblk_ref, ar_ref, ai_ref, cblk_ref, d_ref,
               wglu_ref, bglu_ref, wout_ref, o_ref, bre, bim, st_re, st_im):
    tm = S5_T * BATCH

    @pl.when(pl.program_id(0) == 0)
    def _():
        st_re[...] = jnp.zeros_like(st_re)
        st_im[...] = jnp.zeros_like(st_im)

    x3 = jnp.swapaxes(x_ref[...], 0, 1)
    mod = mod_ref[...]
    shift = mod[:, :D_MODEL]
    scale = mod[:, D_MODEL:2 * D_MODEL]
    gate = mod[:, 2 * D_MODEL:]
    ms = jnp.mean(x3 * x3, axis=-1, keepdims=True)
    h3 = (x3 * lax.rsqrt(ms + EPS)) * (g_ref[...] * (1.0 + scale))[None] + shift[None]
    h = h3.reshape(tm, D_MODEL).astype(BF16)
    uz = jnp.dot(h, win_ref[...], preferred_element_type=F32)
    u = uz[:, :D_MODEL]
    z = uz[:, D_MODEL:]
    ub = u.astype(BF16)

    for c in range(N_KC):
        bu = jnp.dot(ub[:, c * S5_KC:(c + 1) * S5_KC], bblk_ref[c], preferred_element_type=F32)
        bre[2 * c] = bu[:, 0:SCAN_W]
        bre[2 * c + 1] = bu[:, SCAN_W:2 * SCAN_W]
        bim[2 * c] = bu[:, KC_STATES:KC_STATES + SCAN_W]
        bim[2 * c + 1] = bu[:, KC_STATES + SCAN_W:]

    for j in range(N_SCAN):
        ar = ar_ref[j]
        ai = ai_ref[j]
        sr = st_re[j]
        si = st_im[j]
        for t in range(S5_T):
            rows = pl.ds(t * BATCH, BATCH)
            sr, si = (ar * sr - ai * si + bre[j, rows, :],
                      ar * si + ai * sr + bim[j, rows, :])
            bre[j, rows, :] = sr
            bim[j, rows, :] = si
        st_re[j] = sr
        st_im[j] = si

    ys = []
    for c in range(N_KC):
        acc = jnp.dot(bre[2 * c].astype(BF16), cblk_ref[c, 0:SCAN_W, :], preferred_element_type=F32)
        acc += jnp.dot(bre[2 * c + 1].astype(BF16), cblk_ref[c, SCAN_W:2 * SCAN_W, :],
                       preferred_element_type=F32)
        acc += jnp.dot(bim[2 * c].astype(BF16), cblk_ref[c, KC_STATES:KC_STATES + SCAN_W, :],
                       preferred_element_type=F32)
        acc += jnp.dot(bim[2 * c + 1].astype(BF16), cblk_ref[c, KC_STATES + SCAN_W:, :],
                       preferred_element_type=F32)
        ys.append(acc)
    y = jnp.concatenate(ys, axis=1) + d_ref[...] * u
    y = jax.nn.gelu(y)
    gl = jnp.dot(y.astype(BF16), wglu_ref[...], preferred_element_type=F32) + bglu_ref[...]
    y = y * jax.nn.sigmoid(gl)
    y = y * (z * jax.nn.sigmoid(z))
    o = jnp.dot(y.astype(BF16), wout_ref[...], preferred_element_type=F32)
    out3 = x3 + gate[None] * o.reshape(S5_T, BATCH, D_MODEL)
    o_ref[...] = jnp.swapaxes(out3, 0, 1)


def _s5_layer(x, mod, g, w_in, bblk, ar, ai, cblk, dvec, w_glu, b_glu, w_out):
    tm = S5_T * BATCH
    return pl.pallas_call(
        _s5_kernel,
        grid=(SEQ // S5_T,),
        in_specs=[
            pl.BlockSpec((BATCH, S5_T, D_MODEL), lambda i: (0, i, 0)),
            _const_spec((BATCH, 3 * D_MODEL)),
            _const_spec((1, D_MODEL)),
            _const_spec((D_MODEL, 2 * D_MODEL)),
            _const_spec((N_KC, S5_KC, 2 * KC_STATES)),
            _const_spec((N_SCAN, BATCH, SCAN_W)),
            _const_spec((N_SCAN, BATCH, SCAN_W)),
            _const_spec((N_KC, 2 * KC_STATES, S5_KC)),
            _const_spec((1, D_MODEL)),
            _const_spec((D_MODEL, D_MODEL)),
            _const_spec((1, D_MODEL)),
            _const_spec((D_MODEL, D_MODEL)),
        ],
        out_specs=pl.BlockSpec((BATCH, S5_T, D_MODEL), lambda i: (0, i, 0)),
        out_shape=jax.ShapeDtypeStruct((BATCH, SEQ, D_MODEL), F32),
        scratch_shapes=[
            pltpu.VMEM((N_SCAN, tm, SCAN_W), F32),
            pltpu.VMEM((N_SCAN, tm, SCAN_W), F32),
            pltpu.VMEM((N_SCAN, BATCH, SCAN_W), F32),
            pltpu.VMEM((N_SCAN, BATCH, SCAN_W), F32),
        ],
        compiler_params=_cparams(("arbitrary",)),
        name="s5_layer",
    )(x, mod, g, w_in, bblk, ar, ai, cblk, dvec, w_glu, b_glu, w_out)


def _s5_params(log_dt, a_re, a_im, b_re, b_im, c_re, c_im):
    dt = jnp.exp(log_dt)[:, None]
    mag = jnp.exp(a_re * dt)
    abar_r, abar_i = mag * jnp.cos(a_im * dt), mag * jnp.sin(a_im * dt)
    den = a_re * a_re + a_im * a_im
    nr = abar_r - 1.0
    coef_r = (nr * a_re + abar_i * a_im) / den
    coef_i = (abar_i * a_re - nr * a_im) / den
    bb_r = coef_r[..., None] * b_re - coef_i[..., None] * b_im
    bb_i = coef_r[..., None] * b_im + coef_i[..., None] * b_re
    gl = S5_KC // GROUP

    def block_diag(t):
        rows_blk = jnp.arange(t.shape[1]) // (t.shape[1] // gl)
        cols_blk = jnp.arange(gl * t.shape[2]) // t.shape[2]
        return jnp.where(rows_blk[:, None] == cols_blk[None, :], jnp.tile(t, (1, 1, gl)), 0.0)

    def pack_b(bb):
        return block_diag(bb.transpose(0, 2, 1).reshape(N_KC, S5_KC, STATE))

    def pack_c(cc):
        return block_diag(cc.transpose(0, 2, 1).reshape(N_KC, KC_STATES, GROUP))

    bblk = jnp.concatenate([pack_b(bb_r), pack_b(bb_i)], axis=2).astype(BF16)
    cblk = jnp.concatenate([pack_c(c_re), pack_c(-c_im)], axis=1).astype(BF16)

    def rows(a):
        return jnp.broadcast_to(a.reshape(N_SCAN, 1, SCAN_W), (N_SCAN, BATCH, SCAN_W))

    return bblk, cblk, rows(abar_r), rows(abar_i)


def _log_sigmoid(x):
    return jnp.minimum(x, 0.0) - jnp.log1p(jnp.exp(-jnp.abs(x)))


def _proj_kernel(x_ref, mkv_ref, mb_ref, gkv_ref, gb_ref, wk_ref, wvt_ref, wfh_ref, wfl_ref, fb_ref,
                 wq_ref, wzt_ref, kng_ref, qng_ref, part_ref,
                 kaug_ref, qaug_ref, vt_ref, szt_ref, ft_ref, carry_ref):
    tm = PROJ_TM

    @pl.when(pl.program_id(1) == 0)
    def _():
        carry_ref[...] = jnp.zeros_like(carry_ref)

    x = x_ref[...]
    xn = x * lax.rsqrt(jnp.mean(x * x, axis=-1, keepdims=True) + EPS)
    mkv = mkv_ref[...]
    h2 = xn * (gkv_ref[...] * (1.0 + mkv[:, D_MODEL:])) + mkv[:, :D_MODEL]
    mb = mb_ref[...]
    h3 = xn * (gb_ref[...] * (1.0 + mb[:, D_MODEL:2 * D_MODEL])) + mb[:, :D_MODEL]
    h2b = h2.astype(BF16)
    h3b = h3.astype(BF16)
    trans_b = (((1,), (1,)), ((), ()))
    k = jnp.dot(h2b, wk_ref[...], preferred_element_type=F32)
    vt = lax.dot_general(wvt_ref[...], h2b, trans_b, preferred_element_type=F32)
    q = jnp.dot(h3b, wq_ref[...], preferred_element_type=F32)
    zt = lax.dot_general(wzt_ref[...], h3b, trans_b, preferred_element_type=F32)

    h2l = (h2 - h2b.astype(F32)).astype(BF16)
    f = (jnp.dot(h2b, wfh_ref[...], preferred_element_type=F32)
         + jnp.dot(h2b, wfl_ref[...], preferred_element_type=F32)
         + jnp.dot(h2l, wfh_ref[...], preferred_element_type=F32)) + fb_ref[...]
    ls = _log_sigmoid(f)
    ri = lax.broadcasted_iota(jnp.int32, (tm, tm), 0)
    ci = lax.broadcasted_iota(jnp.int32, (tm, tm), 1)
    tri = jnp.where(ci <= ri, 1.0, 0.0).astype(BF16)
    l_hi, l_mid, l_lo = _split3(ls)
    fcum = (jnp.dot(tri, l_hi.astype(BF16), preferred_element_type=F32)
            + jnp.dot(tri, l_mid.astype(BF16), preferred_element_type=F32)
            + jnp.dot(tri, l_lo.astype(BF16), preferred_element_type=F32)) + carry_ref[...]
    carry_ref[...] = fcum[tm - 1:tm, :]

    f2 = fcum * LOG2E
    fct = f2.T
    for h in range(N_HEADS):
        ft_ref[h] = fct[3 * h:3 * h + 1, :]
    n_hi, n_mid, n_lo = _split3(-f2)
    part = part_ref[...]
    f_parts = jnp.where(part == 0, n_hi, jnp.where(part == 1, n_mid, n_lo))

    lane = lax.broadcasted_iota(jnp.int32, (tm, LANES), 1)
    low = lane < HEAD_DIM
    ones_cols = jnp.where(lane < BIAS_LANE + 3, 1.0, 0.0)
    kng = kng_ref[...]
    qng = qng_ref[...]

    def pair_scale(sq):
        ss_a = jnp.sum(jnp.where(low, sq, 0.0), axis=-1, keepdims=True)
        ss_b = jnp.sum(jnp.where(low, 0.0, sq), axis=-1, keepdims=True)
        return lax.rsqrt(jnp.where(low, ss_a, ss_b) * (1.0 / HEAD_DIM) + EPS)

    for hp in range(N_HEADS // 2):
        kp = k[:, hp * LANES:(hp + 1) * LANES]
        qp = q[:, hp * LANES:(hp + 1) * LANES]
        knp = (kp * pair_scale(kp * kp)) * kng
        qnp = (qp * pair_scale(qp * qp)) * qng
        for half in range(2):
            h = 2 * hp + half
            ka = knp if half == 0 else pltpu.roll(knp, HEAD_DIM, 1)
            qa = qnp if half == 0 else pltpu.roll(qnp, HEAD_DIM, 1)
            bias = pltpu.roll(f_parts, BIAS_LANE - 3 * h, 1)
            bias = jnp.where(lane < BIAS_LANE + 3, bias, 0.0)
            kaug_ref[h] = jnp.where(low, ka, bias).astype(BF16)
            qaug_ref[h] = jnp.where(low, qa, ones_cols).astype(BF16)

    vt3 = vt.reshape(N_HEADS, HEAD_DIM, tm).astype(BF16)
    pad_row = lax.broadcasted_iota(jnp.int32, (N_HEADS, V_ROWS - HEAD_DIM, ATT_T), 1)
    ones_rows = jnp.where(pad_row == 0, 1.0, 0.0).astype(BF16)
    for kt in range(KT_PER_PROJ):
        vt_ref[:, kt, 0:HEAD_DIM, :] = vt3[:, :, kt * ATT_T:(kt + 1) * ATT_T]
        vt_ref[:, kt, HEAD_DIM:, :] = ones_rows
    szt_ref[...] = (zt * jax.nn.sigmoid(zt)).astype(BF16)


def _fox_proj(x1, mkv, mb, gkv, gb, wk, wvt, wfh, wfl, fb, wq, wzt, kng, qng, part):
    tm = PROJ_TM
    nt = SEQ // tm
    row = lambda b, t: (b, t, 0)
    per_b = lambda b, t: (b, 0, 0)
    return pl.pallas_call(
        _proj_kernel,
        grid=(BATCH, nt),
        in_specs=[
            pl.BlockSpec((None, tm, D_MODEL), row),
            pl.BlockSpec((None, 1, 2 * D_MODEL), per_b),
            pl.BlockSpec((None, 1, 3 * D_MODEL), per_b),
            _const_spec((1, D_MODEL)),
            _const_spec((1, D_MODEL)),
            _const_spec((D_MODEL, D_MODEL)),
            _const_spec((D_MODEL, D_MODEL)),
            _const_spec((D_MODEL, LANES)),
            _const_spec((D_MODEL, LANES)),
            _const_spec((1, LANES)),
            _const_spec((D_MODEL, D_MODEL)),
            _const_spec((D_MODEL, D_MODEL)),
            _const_spec((1, LANES)),
            _const_spec((1, LANES)),
            _const_spec((1, LANES)),
        ],
        out_specs=[
            pl.BlockSpec((None, N_HEADS, tm, LANES), lambda b, t: (b, 0, t, 0)),
            pl.BlockSpec((None, N_HEADS, tm, LANES), lambda b, t: (b, 0, t, 0)),
            pl.BlockSpec((None, N_HEADS, KT_PER_PROJ, V_ROWS, ATT_T), lambda b, t: (b, 0, t, 0, 0)),
            pl.BlockSpec((None, D_MODEL, tm), lambda b, t: (b, 0, t)),
            pl.BlockSpec((None, N_HEADS, 1, tm), lambda b, t: (b, 0, 0, t)),
        ],
        out_shape=[
            jax.ShapeDtypeStruct((BATCH, N_HEADS, SEQ, LANES), BF16),
            jax.ShapeDtypeStruct((BATCH, N_HEADS, SEQ, LANES), BF16),
            jax.ShapeDtypeStruct((BATCH, N_HEADS, N_KT, V_ROWS, ATT_T), BF16),
            jax.ShapeDtypeStruct((BATCH, D_MODEL, SEQ), BF16),
            jax.ShapeDtypeStruct((BATCH, N_HEADS, 1, SEQ), F32),
        ],
        scratch_shapes=[pltpu.VMEM((1, LANES), F32)],
        compiler_params=_cparams(("arbitrary", "arbitrary")),
        name="fox_proj",
    )(x1, mkv, mb, gkv, gb, wk, wvt, wfh, wfl, fb, wq, wzt, kng, qng, part)


def _attn_kernel(q_ref, k_ref, v_ref, fq_ref, szt_ref, x_ref, mb_ref, w_ref, o_ref,
                 acc_ref, m_ref, ot_ref, s_ref):
    t = ATT_T
    qi = pl.program_id(1)
    trans_b = (((1,), (1,)), ((), ()))
    ki = lax.broadcasted_iota(jnp.int32, (t, t), 0)
    qq = lax.broadcasted_iota(jnp.int32, (t, t), 1)
    visible = ki <= qq

    def scores(i, h, kj, diagonal):
        k = k_ref[h, pl.ds(pl.multiple_of(kj * t, t), t), :]
        s = lax.dot_general(k, q_ref[h], trans_b, preferred_element_type=F32)
        if diagonal:
            s = jnp.where(visible, s, -jnp.inf)
        s_ref[i] = s
        return jnp.max(s, axis=0, keepdims=True)

    def accumulate(i, h, kj, s_max):
        fq = fq_ref[h]
        m_old = m_ref[i]
        m_new = jnp.maximum(m_old, s_max + fq)
        alpha = jnp.exp2(m_old - m_new)
        p = jnp.exp2(s_ref[i] + (fq - m_new)).astype(BF16)
        acc_ref[i] = alpha * acc_ref[i] + jnp.dot(v_ref[h, kj], p, preferred_element_type=F32)
        m_ref[i] = m_new

    def group_body(hg, _):
        heads = [(i, hg * HEAD_GROUP + i) for i in range(HEAD_GROUP)]
        for i, _h in heads:
            m_ref[i] = jnp.full((1, t), -jnp.inf, F32)
            acc_ref[i] = jnp.zeros((V_ROWS, t), F32)

        def blocks(kj, diagonal):
            s_max = [scores(i, h, kj, diagonal) for i, h in heads]
            for (i, h), sm in zip(heads, s_max):
                accumulate(i, h, kj, sm)

        def k_body(kj, _):
            blocks(kj, False)
            return 0

        lax.fori_loop(0, qi, k_body, 0)
        blocks(qi, True)
        for i, h in heads:
            a = acc_ref[i]
            rows = pl.ds(pl.multiple_of(h * HEAD_DIM, HEAD_DIM), HEAD_DIM)
            ot_ref[rows, :] = a[0:HEAD_DIM] * (1.0 / a[HEAD_DIM:HEAD_DIM + 1])
        return 0

    lax.fori_loop(0, N_HEADS // HEAD_GROUP, group_body, 0)

    yt = ot_ref[...] * szt_ref[...].astype(F32)
    y = yt.T.astype(BF16)
    out = jnp.dot(y, w_ref[...], preferred_element_type=F32)
    gate = mb_ref[...][:, 2 * D_MODEL:]
    o_ref[...] = x_ref[...] + gate * out


def _fox_attn(qaug, kaug, vt, ft, szt, x1, mb, w_out):
    t = ATT_T
    return pl.pallas_call(
        _attn_kernel,
        grid=(BATCH, SEQ // t),
        in_specs=[
            pl.BlockSpec((None, N_HEADS, t, LANES), lambda b, i: (b, 0, i, 0)),
            pl.BlockSpec((None, N_HEADS, SEQ, LANES), lambda b, i: (b, 0, 0, 0)),
            pl.BlockSpec((None, N_HEADS, N_KT, V_ROWS, t), lambda b, i: (b, 0, 0, 0, 0)),
            pl.BlockSpec((None, N_HEADS, 1, t), lambda b, i: (b, 0, 0, i)),
            pl.BlockSpec((None, D_MODEL, t), lambda b, i: (b, 0, i)),
            pl.BlockSpec((None, t, D_MODEL), lambda b, i: (b, i, 0)),
            pl.BlockSpec((None, 1, 3 * D_MODEL), lambda b, i: (b, 0, 0)),
            _const_spec((D_MODEL, D_MODEL)),
        ],
        out_specs=pl.BlockSpec((None, t, D_MODEL), lambda b, i: (b, i, 0)),
        out_shape=jax.ShapeDtypeStruct((BATCH, SEQ, D_MODEL), F32),
        scratch_shapes=[
            pltpu.VMEM((HEAD_GROUP, V_ROWS, t), F32),
            pltpu.VMEM((HEAD_GROUP, 1, t), F32),
            pltpu.VMEM((D_MODEL, t), F32),
            pltpu.VMEM((HEAD_GROUP, t, t), F32),
        ],
        compiler_params=_cparams(("arbitrary", "arbitrary")),
        name="fox_attn",
    )(qaug, kaug, vt, ft, szt, x1, mb, w_out)


def kernel(x, c, a_norm_g, a_mod_w, a_mod_b, a_w_in, a_log_dt, a_A_re, a_A_im, a_B_re, a_B_im,
           a_C_re, a_C_im, a_D, a_w_glu, a_b_glu, a_w_out, kv_norm_g, kv_mod_w, kv_mod_b, kv_w,
           kv_f_bias, k_norm_g, b_norm_g, b_mod_w, b_mod_b, b_w_in, q_norm_g, b_w_out):
    assert x.shape == (BATCH, SEQ, D_MODEL) and a_mod_w.shape[0] == 1 and b_mod_w.shape[0] == 1
    aw = N_HEADS * HEAD_DIM

    mod_a = _modulation(c, a_mod_w[0], a_mod_b[0])
    mod_kv = _modulation(c, kv_mod_w, kv_mod_b)
    mod_b = _modulation(c, b_mod_w[0], b_mod_b[0])

    bblk, cblk, ar, ai = _s5_params(a_log_dt[0], a_A_re[0], a_A_im[0], a_B_re[0], a_B_im[0],
                                    a_C_re[0], a_C_im[0])
    x1 = _s5_layer(x, mod_a, a_norm_g[0].reshape(1, D_MODEL), a_w_in[0].astype(BF16), bblk, ar, ai,
                   cblk, a_D[0].reshape(1, D_MODEL), a_w_glu[0].astype(BF16),
                   a_b_glu[0].reshape(1, D_MODEL), a_w_out[0].astype(BF16))

    wk = kv_w[:, :aw].astype(BF16)
    wvt = kv_w[:, aw:2 * aw].T.astype(BF16)
    wf = jnp.pad(jnp.repeat(kv_w[:, 2 * aw:], 3, axis=1), ((0, 0), (0, LANES - 3 * N_HEADS)))
    wfh = wf.astype(BF16)
    wfl = (wf - wfh.astype(F32)).astype(BF16)
    fb = jnp.pad(jnp.repeat(kv_f_bias, 3), (0, LANES - 3 * N_HEADS)).reshape(1, LANES)
    part = (jnp.arange(LANES, dtype=jnp.int32) % 3).reshape(1, LANES)
    wq = b_w_in[0][:, :aw].astype(BF16)
    wzt = b_w_in[0][:, aw:].T.astype(BF16)
    kng = jnp.tile(k_norm_g, 2).reshape(1, LANES)
    qng = (jnp.tile(q_norm_g[0], 2) * (HEAD_DIM ** -0.5 * LOG2E)).reshape(1, LANES)
    mkv3 = mod_kv.reshape(BATCH, 1, 2 * D_MODEL)
    mb3 = mod_b.reshape(BATCH, 1, 3 * D_MODEL)
    kaug, qaug, vt, szt, ft = _fox_proj(x1, mkv3, mb3, kv_norm_g.reshape(1, D_MODEL),
                                        b_norm_g[0].reshape(1, D_MODEL), wk, wvt, wfh, wfl, fb,
                                        wq, wzt, kng, qng, part)
    return _fox_attn(qaug, kaug, vt, ft, szt, x1, mb3, b_w_out[0].astype(BF16))
```

```python
import jax
import jax.numpy as jnp
from jax import lax
from jax.experimental import pallas as pl
from jax.experimental.pallas import tpu as pltpu

D_MODEL = 1024
BATCH = 8
SEQ = 2048
GROUP = 16
N_GROUPS = D_MODEL // GROUP
STATE = 64
N_STATES = N_GROUPS * STATE
N_HEADS = 16
HEAD_DIM = 64
EPS = 1e-6

F32 = jnp.float32
BF16 = jnp.bfloat16

SUBLANES = 8
LANES = 128
MXU_DIM = 256
VMEM_LIMIT_BYTES = 56 * 1024 * 1024

S5_T = 64
S5_PAIRS = S5_T // 2
S5_CH = LANES
N_CH = D_MODEL // S5_CH
CH_GROUPS = S5_CH // GROUP
CH_STATES = CH_GROUPS * STATE
PROJ_TM = 512
ATT_T = 256
N_KT = SEQ // ATT_T
KT_PER_PROJ = PROJ_TM // ATT_T
HEAD_GROUP = 16
V_ROWS = HEAD_DIM + 16
BIAS_LANE = HEAD_DIM
MOD_BN = 512
LOG2E = 1.4426950408889634


def _cparams(sem):
    return pltpu.CompilerParams(dimension_semantics=sem, vmem_limit_bytes=VMEM_LIMIT_BYTES)


def _const_spec(shape):
    nd = len(shape)
    return pl.BlockSpec(shape, lambda *_: (0,) * nd, pipeline_mode=pl.Buffered(1))


def _split3(x):
    hi = x.astype(BF16).astype(F32)
    r = x - hi
    mid = r.astype(BF16).astype(F32)
    lo = (r - mid).astype(BF16).astype(F32)
    return hi, mid, lo


def _mod_kernel(c_ref, w_ref, b_ref, o_ref):
    c = c_ref[...]
    s = (c * jax.nn.sigmoid(c)).astype(BF16)
    o_ref[...] = jnp.dot(s, w_ref[...].astype(BF16), preferred_element_type=F32) + b_ref[...]


def _modulation(c, w, b):
    n = w.shape[1]
    return pl.pallas_call(
        _mod_kernel,
        grid=(n // MOD_BN,),
        in_specs=[
            pl.BlockSpec((BATCH, D_MODEL), lambda j: (0, 0)),
            pl.BlockSpec((D_MODEL, MOD_BN), lambda j: (0, j)),
            pl.BlockSpec((1, MOD_BN), lambda j: (0, j)),
        ],
        out_specs=pl.BlockSpec((BATCH, MOD_BN), lambda j: (0, j)),
        out_shape=jax.ShapeDtypeStruct((BATCH, n), F32),
        compiler_params=_cparams(("arbitrary",)),
        name="modulation",
    )(c, w, b.reshape(1, n))


def _s5_kernel(x_ref, mod_ref, g_ref, win_ref, w2_ref, a2r_ref, a2i_ref, wc_ref, cb_ref, d_ref,
               wglu_ref, bglu_ref, wout_ref, o_ref, sre, sim, st_re, st_im):
    tm = S5_T * BATCH
    mh = S5_PAIRS * BATCH

    @pl.when(pl.program_id(0) == 0)
    def _():
        st_re[...] = jnp.zeros_like(st_re)
        st_im[...] = jnp.zeros_like(st_im)

    x4 = jnp.swapaxes(x_ref[...], 0, 1).reshape(S5_PAIRS, 2, BATCH, D_MODEL)
    x3 = jnp.concatenate([x4[:, 0], x4[:, 1]], axis=0)
    mod = mod_ref[...]
    shift = mod[:, :D_MODEL]
    scale = mod[:, D_MODEL:2 * D_MODEL]
    gate = mod[:, 2 * D_MODEL:]
    ms = jnp.mean(x3 * x3, axis=-1, keepdims=True)
    h3 = (x3 * lax.rsqrt(ms + EPS)) * (g_ref[...] * (1.0 + scale))[None] + shift[None]
    h = h3.reshape(tm, D_MODEL).astype(BF16)
    uz = jnp.dot(h, win_ref[...], preferred_element_type=F32)
    u = uz[:, :D_MODEL]
    z = uz[:, D_MODEL:]
    ub = u.astype(BF16)
    ue = ub[:mh]
    uo = ub[mh:]

    for c in range(N_CH):
        cols = slice(c * S5_CH, (c + 1) * S5_CH)
        lhs = jnp.concatenate([ue[:, cols], uo[:, cols]], axis=1)
        p = jnp.dot(lhs, w2_ref[c], preferred_element_type=F32)
        sre[c, BATCH:, :] = p[:, :CH_STATES]
        sim[c, BATCH:, :] = p[:, CH_STATES:]

    for c in range(N_CH):
        a2r = a2r_ref[c]
        a2i = a2i_ref[c]
        sr = st_re[c]
        si = st_im[c]
        sre[c, 0:BATCH, :] = sr
        sim[c, 0:BATCH, :] = si
        for m in range(S5_PAIRS):
            rows = pl.ds((m + 1) * BATCH, BATCH)
            sr, si = (a2r * sr - a2i * si + sre[c, rows, :],
                      a2r * si + a2i * sr + sim[c, rows, :])
            sre[c, rows, :] = sr
            sim[c, rows, :] = si
        st_re[c] = sr
        st_im[c] = si

    y_even, y_odd = [], []
    for c in range(N_CH):
        res = (jnp.dot(sre[c].astype(BF16), wc_ref[c, :CH_STATES, :], preferred_element_type=F32)
               + jnp.dot(sim[c].astype(BF16), wc_ref[c, CH_STATES:, :], preferred_element_type=F32))
        direct = jnp.dot(ue[:, c * S5_CH:(c + 1) * S5_CH], cb_ref[c], preferred_element_type=F32)
        y_odd.append(res[BATCH:, :S5_CH])
        y_even.append(res[:mh, S5_CH:] + direct)
    y = jnp.concatenate([jnp.concatenate(y_even, axis=1), jnp.concatenate(y_odd, axis=1)], axis=0)
    y = y + d_ref[...] * u
    y = jax.nn.gelu(y)
    gl = jnp.dot(y.astype(BF16), wglu_ref[...], preferred_element_type=F32) + bglu_ref[...]
    y = y * jax.nn.sigmoid(gl)
    y = y * (z * jax.nn.sigmoid(z))
    o = jnp.dot(y.astype(BF16), wout_ref[...], preferred_element_type=F32)
    out3 = x3 + gate[None] * o.reshape(S5_T, BATCH, D_MODEL)
    out3 = jnp.stack([out3[:S5_PAIRS], out3[S5_PAIRS:]], axis=1).reshape(S5_T, BATCH, D_MODEL)
    o_ref[...] = jnp.swapaxes(out3, 0, 1)


def _s5_layer(x, mod, g, w_in, w2, a2r, a2i, wc, cb, dvec, w_glu, b_glu, w_out):
    buf_rows = (S5_PAIRS + 1) * BATCH
    return pl.pallas_call(
        _s5_kernel,
        grid=(SEQ // S5_T,),
        in_specs=[
            pl.BlockSpec((BATCH, S5_T, D_MODEL), lambda i: (0, i, 0)),
            _const_spec((BATCH, 3 * D_MODEL)),
            _const_spec((1, D_MODEL)),
            _const_spec((D_MODEL, 2 * D_MODEL)),
            _const_spec((N_CH, 2 * S5_CH, 2 * CH_STATES)),
            _const_spec((N_CH, BATCH, CH_STATES)),
            _const_spec((N_CH, BATCH, CH_STATES)),
            _const_spec((N_CH, 2 * CH_STATES, 2 * S5_CH)),
            _const_spec((N_CH, S5_CH, S5_CH)),
            _const_spec((1, D_MODEL)),
            _const_spec((D_MODEL, D_MODEL)),
            _const_spec((1, D_MODEL)),
            _const_spec((D_MODEL, D_MODEL)),
        ],
        out_specs=pl.BlockSpec((BATCH, S5_T, D_MODEL), lambda i: (0, i, 0)),
        out_shape=jax.ShapeDtypeStruct((BATCH, SEQ, D_MODEL), F32),
        scratch_shapes=[
            pltpu.VMEM((N_CH, buf_rows, CH_STATES), F32),
            pltpu.VMEM((N_CH, buf_rows, CH_STATES), F32),
            pltpu.VMEM((N_CH, BATCH, CH_STATES), F32),
            pltpu.VMEM((N_CH, BATCH, CH_STATES), F32),
        ],
        compiler_params=_cparams(("arbitrary",)),
        name="s5_layer",
    )(x, mod, g, w_in, w2, a2r, a2i, wc, cb, dvec, w_glu, b_glu, w_out)


def _s5_params(log_dt, a_re, a_im, b_re, b_im, c_re, c_im):
    dt = jnp.exp(log_dt)[:, None]
    mag = jnp.exp(a_re * dt)
    ar, ai = mag * jnp.cos(a_im * dt), mag * jnp.sin(a_im * dt)
    den = a_re * a_re + a_im * a_im
    nr = ar - 1.0
    coef_r = (nr * a_re + ai * a_im) / den
    coef_i = (ai * a_re - nr * a_im) / den
    bb_r = coef_r[..., None] * b_re - coef_i[..., None] * b_im
    bb_i = coef_r[..., None] * b_im + coef_i[..., None] * b_re
    a2r, a2i = ar * ar - ai * ai, 2.0 * ar * ai
    ab_r = ar[..., None] * bb_r - ai[..., None] * bb_i
    ab_i = ar[..., None] * bb_i + ai[..., None] * bb_r
    ca_r = c_re * ar[:, None, :] - c_im * ai[:, None, :]
    ca_i = c_re * ai[:, None, :] + c_im * ar[:, None, :]
    hi = lax.Precision.HIGHEST
    cb = (jnp.einsum('gcp,gpk->gck', c_re, bb_r, precision=hi)
          - jnp.einsum('gcp,gpk->gck', c_im, bb_i, precision=hi))

    def block_diag(t):
        r, q = t.shape[1], t.shape[2]
        t = t.reshape(N_CH, CH_GROUPS * r, q)
        rows_blk = jnp.arange(CH_GROUPS * r) // r
        cols_blk = jnp.arange(CH_GROUPS * q) // q
        return jnp.where(rows_blk[:, None] == cols_blk[None, :], jnp.tile(t, (1, 1, CH_GROUPS)), 0.0)

    def in_rows(b_r, b_i):
        return jnp.concatenate([block_diag(b_r.transpose(0, 2, 1)), block_diag(b_i.transpose(0, 2, 1))],
                               axis=2)

    def out_cols(c_r, c_i):
        return jnp.concatenate([block_diag(c_r.transpose(0, 2, 1)), block_diag(-c_i.transpose(0, 2, 1))],
                               axis=1)

    w2 = jnp.concatenate([in_rows(ab_r, ab_i), in_rows(bb_r, bb_i)], axis=1).astype(BF16)
    wc = jnp.concatenate([out_cols(c_re, c_im), out_cols(ca_r, ca_i)], axis=2).astype(BF16)
    cbp = block_diag(cb.transpose(0, 2, 1)).astype(BF16)

    def rows(a):
        return jnp.broadcast_to(a.reshape(N_CH, 1, CH_STATES), (N_CH, BATCH, CH_STATES))

    return w2, wc, cbp, rows(a2r), rows(a2i)


def _log_sigmoid(x):
    return jnp.minimum(x, 0.0) - jnp.log1p(jnp.exp(-jnp.abs(x)))


def _proj_kernel(x_ref, mkv_ref, mb_ref, gkv_ref, gb_ref, wk_ref, wvt_ref, wfh_ref, wfl_ref, fb_ref,
                 wq_ref, wzt_ref, kng_ref, qng_ref, part_ref,
                 kaug_ref, qaug_ref, vt_ref, szt_ref, ft_ref, carry_ref):
    tm = PROJ_TM

    @pl.when(pl.program_id(1) == 0)
    def _():
        carry_ref[...] = jnp.zeros_like(carry_ref)

    x = x_ref[...]
    xn = x * lax.rsqrt(jnp.mean(x * x, axis=-1, keepdims=True) + EPS)
    mkv = mkv_ref[...]
    h2 = xn * (gkv_ref[...] * (1.0 + mkv[:, D_MODEL:])) + mkv[:, :D_MODEL]
    mb = mb_ref[...]
    h3 = xn * (gb_ref[...] * (1.0 + mb[:, D_MODEL:2 * D_MODEL])) + mb[:, :D_MODEL]
    h2b = h2.astype(BF16)
    h3b = h3.astype(BF16)
    trans_b = (((1,), (1,)), ((), ()))
    k = jnp.dot(h2b, wk_ref[...], preferred_element_type=F32)
    vt = lax.dot_general(wvt_ref[...], h2b, trans_b, preferred_element_type=F32)
    q = jnp.dot(h3b, wq_ref[...], preferred_element_type=F32)
    zt = lax.dot_general(wzt_ref[...], h3b, trans_b, preferred_element_type=F32)

    h2l = (h2 - h2b.astype(F32)).astype(BF16)
    f = (jnp.dot(h2b, wfh_ref[...], preferred_element_type=F32)
         + jnp.dot(h2b, wfl_ref[...], preferred_element_type=F32)
         + jnp.dot(h2l, wfh_ref[...], preferred_element_type=F32)) + fb_ref[...]
    ls = _log_sigmoid(f)
    ri = lax.broadcasted_iota(jnp.int32, (tm, tm), 0)
    ci = lax.broadcasted_iota(jnp.int32, (tm, tm), 1)
    tri = jnp.where(ci <= ri, 1.0, 0.0).astype(BF16)
    l_hi, l_mid, l_lo = _split3(ls)
    fcum = (jnp.dot(tri, l_hi.astype(BF16), preferred_element_type=F32)
            + jnp.dot(tri, l_mid.astype(BF16), preferred_element_type=F32)
            + jnp.dot(tri, l_lo.astype(BF16), preferred_element_type=F32)) + carry_ref[...]
    carry_ref[...] = fcum[tm - 1:tm, :]

    f2 = fcum * LOG2E
    fct = f2.T
    for h in range(N_HEADS):
        ft_ref[h] = fct[3 * h:3 * h + 1, :]
    n_hi, n_mid, n_lo = _split3(-f2)
    part = part_ref[...]
    f_parts = jnp.where(part == 0, n_hi, jnp.where(part == 1, n_mid, n_lo))

    lane = lax.broadcasted_iota(jnp.int32, (tm, LANES), 1)
    low = lane < HEAD_DIM
    ones_cols = jnp.where(lane < BIAS_LANE + 3, 1.0, 0.0)
    kng = kng_ref[...]
    qng = qng_ref[...]

    def pair_scale(sq):
        ss_a = jnp.sum(jnp.where(low, sq, 0.0), axis=-1, keepdims=True)
        ss_b = jnp.sum(jnp.where(low, 0.0, sq), axis=-1, keepdims=True)
        return lax.rsqrt(jnp.where(low, ss_a, ss_b) * (1.0 / HEAD_DIM) + EPS)

    for hp in range(N_HEADS // 2):
        kp = k[:, hp * LANES:(hp + 1) * LANES]
        qp = q[:, hp * LANES:(hp + 1) * LANES]
        knp = (kp * pair_scale(kp * kp)) * kng
        qnp = (qp * pair_scale(qp * qp)) * qng
        for half in range(2):
            h = 2 * hp + half
            ka = knp if half == 0 else pltpu.roll(knp, HEAD_DIM, 1)
            qa = qnp if half == 0 else pltpu.roll(qnp, HEAD_DIM, 1)
            bias = pltpu.roll(f_parts, BIAS_LANE - 3 * h, 1)
            bias = jnp.where(lane < BIAS_LANE + 3, bias, 0.0)
            kaug_ref[h] = jnp.where(low, ka, bias).astype(BF16)
            qaug_ref[h] = jnp.where(low, qa, ones_cols).astype(BF16)

    vt3 = vt.reshape(N_HEADS, HEAD_DIM, tm).astype(BF16)
    pad_row = lax.broadcasted_iota(jnp.int32, (N_HEADS, V_ROWS - HEAD_DIM, ATT_T), 1)
    ones_rows = jnp.where(pad_row == 0, 1.0, 0.0).astype(BF16)
    for kt in range(KT_PER_PROJ):
        vt_ref[:, kt, 0:HEAD_DIM, :] = vt3[:, :, kt * ATT_T:(kt + 1) * ATT_T]
        vt_ref[:, kt, HEAD_DIM:, :] = ones_rows
    szt_ref[...] = (zt * jax.nn.sigmoid(zt)).astype(BF16)


def _fox_proj(x1, mkv, mb, gkv, gb, wk, wvt, wfh, wfl, fb, wq, wzt, kng, qng, part):
    tm = PROJ_TM
    row = lambda b, t: (b, t, 0)
    per_b = lambda b, t: (b, 0, 0)
    return pl.pallas_call(
        _proj_kernel,
        grid=(BATCH, SEQ // tm),
        in_specs=[
            pl.BlockSpec((None, tm, D_MODEL), row),
            pl.BlockSpec((None, 1, 2 * D_MODEL), per_b),
            pl.BlockSpec((None, 1, 3 * D_MODEL), per_b),
            _const_spec((1, D_MODEL)),
            _const_spec((1, D_MODEL)),
            _const_spec((D_MODEL, D_MODEL)),
            _const_spec((D_MODEL, D_MODEL)),
            _const_spec((D_MODEL, LANES)),
            _const_spec((D_MODEL, LANES)),
            _const_spec((1, LANES)),
            _const_spec((D_MODEL, D_MODEL)),
            _const_spec((D_MODEL, D_MODEL)),
            _const_spec((1, LANES)),
            _const_spec((1, LANES)),
            _const_spec((1, LANES)),
        ],
        out_specs=[
            pl.BlockSpec((None, N_HEADS, tm, LANES), lambda b, t: (b, 0, t, 0)),
            pl.BlockSpec((None, N_HEADS, tm, LANES), lambda b, t: (b, 0, t, 0)),
            pl.BlockSpec((None, N_HEADS, KT_PER_PROJ, V_ROWS, ATT_T), lambda b, t: (b, 0, t, 0, 0)),
            pl.BlockSpec((None, D_MODEL, tm), lambda b, t: (b, 0, t)),
            pl.BlockSpec((None, N_HEADS, 1, tm), lambda b, t: (b, 0, 0, t)),
        ],
        out_shape=[
            jax.ShapeDtypeStruct((BATCH, N_HEADS, SEQ, LANES), BF16),
            jax.ShapeDtypeStruct((BATCH, N_HEADS, SEQ, LANES), BF16),
            jax.ShapeDtypeStruct((BATCH, N_HEADS, N_KT, V_ROWS, ATT_T), BF16),
            jax.ShapeDtypeStruct((BATCH, D_MODEL, SEQ), BF16),
            jax.ShapeDtypeStruct((BATCH, N_HEADS, 1, SEQ), F32),
        ],
        scratch_shapes=[pltpu.VMEM((1, LANES), F32)],
        compiler_params=_cparams(("arbitrary", "arbitrary")),
        name="fox_proj",
    )(x1, mkv, mb, gkv, gb, wk, wvt, wfh, wfl, fb, wq, wzt, kng, qng, part)


def _attn_kernel(q_ref, k_ref, v_ref, fq_ref, szt_ref, x_ref, mb_ref, w_ref, o_ref,
                 acc_ref, m_ref, ot_ref, s_ref):
    t = ATT_T
    qi = pl.program_id(1)
    trans_b = (((1,), (1,)), ((), ()))
    ki = lax.broadcasted_iota(jnp.int32, (t, t), 0)
    qq = lax.broadcasted_iota(jnp.int32, (t, t), 1)
    visible = ki <= qq

    def scores(i, h, kj, diagonal):
        k = k_ref[h, pl.ds(pl.multiple_of(kj * t, t), t), :]
        s = lax.dot_general(k, q_ref[h], trans_b, preferred_element_type=F32)
        if diagonal:
            s = jnp.where(visible, s, -jnp.inf)
        s_ref[i] = s
        return jnp.max(s, axis=0, keepdims=True)

    def accumulate(i, h, kj, s_max):
        fq = fq_ref[h]
        m_old = m_ref[i]
        m_new = jnp.maximum(m_old, s_max + fq)
        alpha = jnp.exp2(m_old - m_new)
        p = jnp.exp2(s_ref[i] + (fq - m_new)).astype(BF16)
        acc_ref[i] = alpha * acc_ref[i] + jnp.dot(v_ref[h, kj], p, preferred_element_type=F32)
        m_ref[i] = m_new

    def group_body(hg, _):
        heads = [(i, hg * HEAD_GROUP + i) for i in range(HEAD_GROUP)]
        for i, _h in heads:
            m_ref[i] = jnp.full((1, t), -jnp.inf, F32)
            acc_ref[i] = jnp.zeros((V_ROWS, t), F32)

        def blocks(kj, diagonal):
            s_max = [scores(i, h, kj, diagonal) for i, h in heads]
            for (i, h), sm in zip(heads, s_max):
                accumulate(i, h, kj, sm)

        def k_body(kj, _):
            blocks(kj, False)
            return 0

        lax.fori_loop(0, qi, k_body, 0)
        blocks(qi, True)
        for i, h in heads:
            a = acc_ref[i]
            rows = pl.ds(pl.multiple_of(h * HEAD_DIM, HEAD_DIM), HEAD_DIM)
            ot_ref[rows, :] = a[0:HEAD_DIM] * (1.0 / a[HEAD_DIM:HEAD_DIM + 1])
        return 0

    lax.fori_loop(0, N_HEADS // HEAD_GROUP, group_body, 0)

    yt = ot_ref[...] * szt_ref[...].astype(F32)
    y = yt.T.astype(BF16)
    out = jnp.dot(y, w_ref[...], preferred_element_type=F32)
    gate = mb_ref[...][:, 2 * D_MODEL:]
    o_ref[...] = x_ref[...] + gate * out


def _fox_attn(qaug, kaug, vt, ft, szt, x1, mb, w_out):
    t = ATT_T
    return pl.pallas_call(
        _attn_kernel,
        grid=(BATCH, SEQ // t),
        in_specs=[
            pl.BlockSpec((None, N_HEADS, t, LANES), lambda b, i: (b, 0, i, 0)),
            pl.BlockSpec((None, N_HEADS, SEQ, LANES), lambda b, i: (b, 0, 0, 0)),
            pl.BlockSpec((None, N_HEADS, N_KT, V_ROWS, t), lambda b, i: (b, 0, 0, 0, 0)),
            pl.BlockSpec((None, N_HEADS, 1, t), lambda b, i: (b, 0, 0, i)),
            pl.BlockSpec((None, D_MODEL, t), lambda b, i: (b, 0, i)),
            pl.BlockSpec((None, t, D_MODEL), lambda b, i: (b, i, 0)),
            pl.BlockSpec((None, 1, 3 * D_MODEL), lambda b, i: (b, 0, 0)),
            _const_spec((D_MODEL, D_MODEL)),
        ],
        out_specs=pl.BlockSpec((None, t, D_MODEL), lambda b, i: (b, i, 0)),
        out_shape=jax.ShapeDtypeStruct((BATCH, SEQ, D_MODEL), F32),
        scratch_shapes=[
            pltpu.VMEM((HEAD_GROUP, V_ROWS, t), F32),
            pltpu.VMEM((HEAD_GROUP, 1, t), F32),
            pltpu.VMEM((D_MODEL, t), F32),
            pltpu.VMEM((HEAD_GROUP, t, t), F32),
        ],
        compiler_params=_cparams(("arbitrary", "arbitrary")),
        name="fox_attn",
    )(qaug, kaug, vt, ft, szt, x1, mb, w_out)


def kernel(x, c, a_norm_g, a_mod_w, a_mod_b, a_w_in, a_log_dt, a_A_re, a_A_im, a_B_re, a_B_im,
           a_C_re, a_C_im, a_D, a_w_glu, a_b_glu, a_w_out, kv_norm_g, kv_mod_w, kv_mod_b, kv_w,
           kv_f_bias, k_norm_g, b_norm_g, b_mod_w, b_mod_b, b_w_in, q_norm_g, b_w_out):
    assert x.shape == (BATCH, SEQ, D_MODEL) and a_mod_w.shape[0] == 1 and b_mod_w.shape[0] == 1
    aw = N_HEADS * HEAD_DIM

    mod_a = _modulation(c, a_mod_w[0], a_mod_b[0])
    mod_kv = _modulation(c, kv_mod_w, kv_mod_b)
    mod_b = _modulation(c, b_mod_w[0], b_mod_b[0])

    w2, wc, cb, a2r, a2i = _s5_params(a_log_dt[0], a_A_re[0], a_A_im[0], a_B_re[0], a_B_im[0],
                                      a_C_re[0], a_C_im[0])
    x1 = _s5_layer(x, mod_a, a_norm_g[0].reshape(1, D_MODEL), a_w_in[0].astype(BF16), w2, a2r, a2i,
                   wc, cb, a_D[0].reshape(1, D_MODEL), a_w_glu[0].astype(BF16),
                   a_b_glu[0].reshape(1, D_MODEL), a_w_out[0].astype(BF16))

    wk = kv_w[:, :aw].astype(BF16)
    wvt = kv_w[:, aw:2 * aw].T.astype(BF16)
    wf = jnp.pad(jnp.repeat(kv_w[:, 2 * aw:], 3, axis=1), ((0, 0), (0, LANES - 3 * N_HEADS)))
    wfh = wf.astype(BF16)
    wfl = (wf - wfh.astype(F32)).astype(BF16)
    fb = jnp.pad(jnp.repeat(kv_f_bias, 3), (0, LANES - 3 * N_HEADS)).reshape(1, LANES)
    part = (jnp.arange(LANES, dtype=jnp.int32) % 3).reshape(1, LANES)
    wq = b_w_in[0][:, :aw].astype(BF16)
    wzt = b_w_in[0][:, aw:].T.astype(BF16)
    kng = jnp.tile(k_norm_g, 2).reshape(1, LANES)
    qng = (jnp.tile(q_norm_g[0], 2) * (HEAD_DIM ** -0.5 * LOG2E)).reshape(1, LANES)
    mkv3 = mod_kv.reshape(BATCH, 1, 2 * D_MODEL)
    mb3 = mod_b.reshape(BATCH, 1, 3 * D_MODEL)
    kaug, qaug, vt, szt, ft = _fox_proj(x1, mkv3, mb3, kv_norm_g.reshape(1, D_MODEL),
                                        b_norm_g[0].reshape(1, D_MODEL), wk, wvt, wfh, wfl, fb,
                                        wq, wzt, kng, qng, part)
    return _fox_attn(qaug, kaug, vt, ft, szt, x1, mb3, b_w_out[0].astype(BF16))
```

```python
import jax
import jax.numpy as jnp
from jax import lax
from jax.experimental import pallas as pl
from jax.experimental.pallas import tpu as pltpu

D_MODEL = 1024
BATCH = 8
SEQ = 2048
GROUP = 16
N_GROUPS = D_MODEL // GROUP
STATE = 64
N_STATES = N_GROUPS * STATE
N_HEADS = 16
HEAD_DIM = 64
EPS = 1e-6

F32 = jnp.float32
BF16 = jnp.bfloat16

SUBLANES = 8
LANES = 128
MXU_DIM = 256
VMEM_LIMIT_BYTES = 56 * 1024 * 1024

S5_T = 64
S5_PAIRS = S5_T // 2
S5_CH = LANES
N_CH = D_MODEL // S5_CH
CH_GROUPS = S5_CH // GROUP
CH_STATES = CH_GROUPS * STATE
PROJ_TM = 512
ATT_T = 256
N_KT = SEQ // ATT_T
KT_PER_PROJ = PROJ_TM // ATT_T
MAX_UNSTABILISED_LOG2 = 64.0
V_ROWS = HEAD_DIM + 16
BIAS_LANE = HEAD_DIM
MOD_BN = 512
LOG2E = 1.4426950408889634


def _cparams(sem):
    return pltpu.CompilerParams(dimension_semantics=sem, vmem_limit_bytes=VMEM_LIMIT_BYTES)


def _const_spec(shape):
    nd = len(shape)
    return pl.BlockSpec(shape, lambda *_: (0,) * nd, pipeline_mode=pl.Buffered(1))


def _split3(x):
    hi = x.astype(BF16).astype(F32)
    r = x - hi
    mid = r.astype(BF16).astype(F32)
    lo = (r - mid).astype(BF16).astype(F32)
    return hi, mid, lo


def _mod_kernel(c_ref, w_ref, b_ref, o_ref):
    c = c_ref[...]
    s = (c * jax.nn.sigmoid(c)).astype(BF16)
    o_ref[...] = jnp.dot(s, w_ref[...].astype(BF16), preferred_element_type=F32) + b_ref[...]


def _modulation(c, w, b):
    n = w.shape[1]
    return pl.pallas_call(
        _mod_kernel,
        grid=(n // MOD_BN,),
        in_specs=[
            pl.BlockSpec((BATCH, D_MODEL), lambda j: (0, 0)),
            pl.BlockSpec((D_MODEL, MOD_BN), lambda j: (0, j)),
            pl.BlockSpec((1, MOD_BN), lambda j: (0, j)),
        ],
        out_specs=pl.BlockSpec((BATCH, MOD_BN), lambda j: (0, j)),
        out_shape=jax.ShapeDtypeStruct((BATCH, n), F32),
        compiler_params=_cparams(("arbitrary",)),
        name="modulation",
    )(c, w, b.reshape(1, n))


def _s5_kernel(x_ref, mod_ref, g_ref, win_ref, w2_ref, a2r_ref, a2i_ref, wc_ref, cb_ref, d_ref,
               wglu_ref, bglu_ref, wout_ref, o_ref, sre, sim, st_re, st_im):
    tm = S5_T * BATCH
    mh = S5_PAIRS * BATCH

    @pl.when(pl.program_id(0) == 0)
    def _():
        st_re[...] = jnp.zeros_like(st_re)
        st_im[...] = jnp.zeros_like(st_im)

    x4 = jnp.swapaxes(x_ref[...], 0, 1).reshape(S5_PAIRS, 2, BATCH, D_MODEL)
    x3 = jnp.concatenate([x4[:, 0], x4[:, 1]], axis=0)
    mod = mod_ref[...]
    shift = mod[:, :D_MODEL]
    scale = mod[:, D_MODEL:2 * D_MODEL]
    gate = mod[:, 2 * D_MODEL:]
    ms = jnp.mean(x3 * x3, axis=-1, keepdims=True)
    h3 = (x3 * lax.rsqrt(ms + EPS)) * (g_ref[...] * (1.0 + scale))[None] + shift[None]
    h = h3.reshape(tm, D_MODEL).astype(BF16)
    uz = jnp.dot(h, win_ref[...], preferred_element_type=F32)
    u = uz[:, :D_MODEL]
    z = uz[:, D_MODEL:]
    ub = u.astype(BF16)
    ue = ub[:mh]
    uo = ub[mh:]

    for c in range(N_CH):
        cols = slice(c * S5_CH, (c + 1) * S5_CH)
        lhs = jnp.concatenate([ue[:, cols], uo[:, cols]], axis=1)
        p = jnp.dot(lhs, w2_ref[c], preferred_element_type=F32)
        sre[c, BATCH:, :] = p[:, :CH_STATES]
        sim[c, BATCH:, :] = p[:, CH_STATES:]

    for c in range(N_CH):
        a2r = a2r_ref[c]
        a2i = a2i_ref[c]
        sr = st_re[c]
        si = st_im[c]
        sre[c, 0:BATCH, :] = sr
        sim[c, 0:BATCH, :] = si
        for m in range(S5_PAIRS):
            rows = pl.ds((m + 1) * BATCH, BATCH)
            sr, si = (a2r * sr - a2i * si + sre[c, rows, :],
                      a2r * si + a2i * sr + sim[c, rows, :])
            sre[c, rows, :] = sr
            sim[c, rows, :] = si
        st_re[c] = sr
        st_im[c] = si

    y_even, y_odd = [], []
    for c in range(N_CH):
        res = (jnp.dot(sre[c].astype(BF16), wc_ref[c, :CH_STATES, :], preferred_element_type=F32)
               + jnp.dot(sim[c].astype(BF16), wc_ref[c, CH_STATES:, :], preferred_element_type=F32))
        direct = jnp.dot(ue[:, c * S5_CH:(c + 1) * S5_CH], cb_ref[c], preferred_element_type=F32)
        y_odd.append(res[BATCH:, :S5_CH])
        y_even.append(res[:mh, S5_CH:] + direct)
    y = jnp.concatenate([jnp.concatenate(y_even, axis=1), jnp.concatenate(y_odd, axis=1)], axis=0)
    y = y + d_ref[...] * u
    y = jax.nn.gelu(y)
    gl = jnp.dot(y.astype(BF16), wglu_ref[...], preferred_element_type=F32) + bglu_ref[...]
    y = y * jax.nn.sigmoid(gl)
    y = y * (z * jax.nn.sigmoid(z))
    o = jnp.dot(y.astype(BF16), wout_ref[...], preferred_element_type=F32)
    out3 = x3 + gate[None] * o.reshape(S5_T, BATCH, D_MODEL)
    out3 = jnp.stack([out3[:S5_PAIRS], out3[S5_PAIRS:]], axis=1).reshape(S5_T, BATCH, D_MODEL)
    o_ref[...] = jnp.swapaxes(out3, 0, 1)


def _s5_layer(x, mod, g, w_in, w2, a2r, a2i, wc, cb, dvec, w_glu, b_glu, w_out):
    buf_rows = (S5_PAIRS + 1) * BATCH
    return pl.pallas_call(
        _s5_kernel,
        grid=(SEQ // S5_T,),
        in_specs=[
            pl.BlockSpec((BATCH, S5_T, D_MODEL), lambda i: (0, i, 0)),
            _const_spec((BATCH, 3 * D_MODEL)),
            _const_spec((1, D_MODEL)),
            _const_spec((D_MODEL, 2 * D_MODEL)),
            _const_spec((N_CH, 2 * S5_CH, 2 * CH_STATES)),
            _const_spec((N_CH, BATCH, CH_STATES)),
            _const_spec((N_CH, BATCH, CH_STATES)),
            _const_spec((N_CH, 2 * CH_STATES, 2 * S5_CH)),
            _const_spec((N_CH, S5_CH, S5_CH)),
            _const_spec((1, D_MODEL)),
            _const_spec((D_MODEL, D_MODEL)),
            _const_spec((1, D_MODEL)),
            _const_spec((D_MODEL, D_MODEL)),
        ],
        out_specs=pl.BlockSpec((BATCH, S5_T, D_MODEL), lambda i: (0, i, 0)),
        out_shape=jax.ShapeDtypeStruct((BATCH, SEQ, D_MODEL), F32),
        scratch_shapes=[
            pltpu.VMEM((N_CH, buf_rows, CH_STATES), F32),
            pltpu.VMEM((N_CH, buf_rows, CH_STATES), F32),
            pltpu.VMEM((N_CH, BATCH, CH_STATES), F32),
            pltpu.VMEM((N_CH, BATCH, CH_STATES), F32),
        ],
        compiler_params=_cparams(("arbitrary",)),
        name="s5_layer",
    )(x, mod, g, w_in, w2, a2r, a2i, wc, cb, dvec, w_glu, b_glu, w_out)


def _s5_params(log_dt, a_re, a_im, b_re, b_im, c_re, c_im):
    dt = jnp.exp(log_dt)[:, None]
    mag = jnp.exp(a_re * dt)
    ar, ai = mag * jnp.cos(a_im * dt), mag * jnp.sin(a_im * dt)
    den = a_re * a_re + a_im * a_im
    nr = ar - 1.0
    coef_r = (nr * a_re + ai * a_im) / den
    coef_i = (ai * a_re - nr * a_im) / den
    bb_r = coef_r[..., None] * b_re - coef_i[..., None] * b_im
    bb_i = coef_r[..., None] * b_im + coef_i[..., None] * b_re
    a2r, a2i = ar * ar - ai * ai, 2.0 * ar * ai
    ab_r = ar[..., None] * bb_r - ai[..., None] * bb_i
    ab_i = ar[..., None] * bb_i + ai[..., None] * bb_r
    ca_r = c_re * ar[:, None, :] - c_im * ai[:, None, :]
    ca_i = c_re * ai[:, None, :] + c_im * ar[:, None, :]
    hi = lax.Precision.HIGHEST
    cb = (jnp.einsum('gcp,gpk->gck', c_re, bb_r, precision=hi)
          - jnp.einsum('gcp,gpk->gck', c_im, bb_i, precision=hi))

    def block_diag(t):
        r, q = t.shape[1], t.shape[2]
        t = t.reshape(N_CH, CH_GROUPS * r, q)
        rows_blk = jnp.arange(CH_GROUPS * r) // r
        cols_blk = jnp.arange(CH_GROUPS * q) // q
        return jnp.where(rows_blk[:, None] == cols_blk[None, :], jnp.tile(t, (1, 1, CH_GROUPS)), 0.0)

    def in_rows(b_r, b_i):
        return jnp.concatenate([block_diag(b_r.transpose(0, 2, 1)), block_diag(b_i.transpose(0, 2, 1))],
                               axis=2)

    def out_cols(c_r, c_i):
        return jnp.concatenate([block_diag(c_r.transpose(0, 2, 1)), block_diag(-c_i.transpose(0, 2, 1))],
                               axis=1)

    w2 = jnp.concatenate([in_rows(ab_r, ab_i), in_rows(bb_r, bb_i)], axis=1).astype(BF16)
    wc = jnp.concatenate([out_cols(c_re, c_im), out_cols(ca_r, ca_i)], axis=2).astype(BF16)
    cbp = block_diag(cb.transpose(0, 2, 1)).astype(BF16)

    def rows(a):
        return jnp.broadcast_to(a.reshape(N_CH, 1, CH_STATES), (N_CH, BATCH, CH_STATES))

    return w2, wc, cbp, rows(a2r), rows(a2i)


def _log_sigmoid(x):
    return jnp.minimum(x, 0.0) - jnp.log1p(jnp.exp(-jnp.abs(x)))


def _proj_kernel(x_ref, mkv_ref, mb_ref, gkv_ref, gb_ref, wk_ref, wvt_ref, wf_ref, fb_ref,
                 wq_ref, wzt_ref, kng_ref, qng_ref, part_ref,
                 kaug_ref, qaug_ref, vt_ref, szt_ref, ft_ref, carry_ref):
    tm = PROJ_TM

    @pl.when(pl.program_id(1) == 0)
    def _():
        carry_ref[...] = jnp.zeros_like(carry_ref)

    x = x_ref[...]
    xn = x * lax.rsqrt(jnp.mean(x * x, axis=-1, keepdims=True) + EPS)
    mkv = mkv_ref[...]
    h2 = xn * (gkv_ref[...] * (1.0 + mkv[:, D_MODEL:])) + mkv[:, :D_MODEL]
    mb = mb_ref[...]
    h3 = xn * (gb_ref[...] * (1.0 + mb[:, D_MODEL:2 * D_MODEL])) + mb[:, :D_MODEL]
    h2b = h2.astype(BF16)
    h3b = h3.astype(BF16)
    trans_b = (((1,), (1,)), ((), ()))
    k = jnp.dot(h2b, wk_ref[...], preferred_element_type=F32)
    vt = lax.dot_general(wvt_ref[...], h2b, trans_b, preferred_element_type=F32)
    q = jnp.dot(h3b, wq_ref[...], preferred_element_type=F32)
    zt = lax.dot_general(wzt_ref[...], h3b, trans_b, preferred_element_type=F32)

    f = jnp.dot(h2b, wf_ref[...], preferred_element_type=F32) + fb_ref[...]
    ls = _log_sigmoid(f)
    ri = lax.broadcasted_iota(jnp.int32, (tm, tm), 0)
    ci = lax.broadcasted_iota(jnp.int32, (tm, tm), 1)
    tri = jnp.where(ci <= ri, 1.0, 0.0).astype(BF16)
    l_hi = ls.astype(BF16)
    l_lo = (ls - l_hi.astype(F32)).astype(BF16)
    fcum = (jnp.dot(tri, l_hi, preferred_element_type=F32)
            + jnp.dot(tri, l_lo, preferred_element_type=F32)) + carry_ref[...]
    carry_ref[...] = fcum[tm - 1:tm, :]

    f2 = fcum * LOG2E
    fct = f2.T
    for h in range(N_HEADS):
        ft_ref[h] = fct[3 * h:3 * h + 1, :]
    n_hi, n_mid, n_lo = _split3(-f2)
    part = part_ref[...]
    f_parts = jnp.where(part == 0, n_hi, jnp.where(part == 1, n_mid, n_lo))

    lane = lax.broadcasted_iota(jnp.int32, (tm, LANES), 1)
    low = lane < HEAD_DIM
    ones_cols = jnp.where(lane < BIAS_LANE + 3, 1.0, 0.0)
    kng = kng_ref[...]
    qng = qng_ref[...]

    def pair_scale(sq):
        ss_a = jnp.sum(jnp.where(low, sq, 0.0), axis=-1, keepdims=True)
        ss_b = jnp.sum(jnp.where(low, 0.0, sq), axis=-1, keepdims=True)
        return lax.rsqrt(jnp.where(low, ss_a, ss_b) * (1.0 / HEAD_DIM) + EPS)

    for hp in range(N_HEADS // 2):
        kp = k[:, hp * LANES:(hp + 1) * LANES]
        qp = q[:, hp * LANES:(hp + 1) * LANES]
        knp = (kp * pair_scale(kp * kp)) * kng
        qnp = (qp * pair_scale(qp * qp)) * qng
        for half in range(2):
            h = 2 * hp + half
            ka = knp if half == 0 else pltpu.roll(knp, HEAD_DIM, 1)
            qa = qnp if half == 0 else pltpu.roll(qnp, HEAD_DIM, 1)
            bias = pltpu.roll(f_parts, BIAS_LANE - 3 * h, 1)
            bias = jnp.where(lane < BIAS_LANE + 3, bias, 0.0)
            kaug_ref[h] = jnp.where(low, ka, bias).astype(BF16)
            qaug_ref[h] = jnp.where(low, qa, ones_cols).astype(BF16)

    vt3 = vt.reshape(N_HEADS, HEAD_DIM, tm).astype(BF16)
    pad_row = lax.broadcasted_iota(jnp.int32, (N_HEADS, V_ROWS - HEAD_DIM, ATT_T), 1)
    ones_rows = jnp.where(pad_row == 0, 1.0, 0.0).astype(BF16)
    for kt in range(KT_PER_PROJ):
        vt_ref[:, kt, 0:HEAD_DIM, :] = vt3[:, :, kt * ATT_T:(kt + 1) * ATT_T]
        vt_ref[:, kt, HEAD_DIM:, :] = ones_rows
    szt_ref[...] = (zt * jax.nn.sigmoid(zt)).astype(BF16)


def _fox_proj(x1, mkv, mb, gkv, gb, wk, wvt, wf, fb, wq, wzt, kng, qng, part):
    tm = PROJ_TM
    row = lambda b, t: (b, t, 0)
    per_b = lambda b, t: (b, 0, 0)
    return pl.pallas_call(
        _proj_kernel,
        grid=(BATCH, SEQ // tm),
        in_specs=[
            pl.BlockSpec((None, tm, D_MODEL), row),
            pl.BlockSpec((None, 1, 2 * D_MODEL), per_b),
            pl.BlockSpec((None, 1, 3 * D_MODEL), per_b),
            _const_spec((1, D_MODEL)),
            _const_spec((1, D_MODEL)),
            _const_spec((D_MODEL, D_MODEL)),
            _const_spec((D_MODEL, D_MODEL)),
            _const_spec((D_MODEL, LANES)),
            _const_spec((1, LANES)),
            _const_spec((D_MODEL, D_MODEL)),
            _const_spec((D_MODEL, D_MODEL)),
            _const_spec((1, LANES)),
            _const_spec((1, LANES)),
            _const_spec((1, LANES)),
        ],
        out_specs=[
            pl.BlockSpec((None, N_HEADS, tm, LANES), lambda b, t: (b, 0, t, 0)),
            pl.BlockSpec((None, N_HEADS, tm, LANES), lambda b, t: (b, 0, t, 0)),
            pl.BlockSpec((None, N_HEADS, KT_PER_PROJ, V_ROWS, ATT_T), lambda b, t: (b, 0, t, 0, 0)),
            pl.BlockSpec((None, D_MODEL, tm), lambda b, t: (b, 0, t)),
            pl.BlockSpec((None, N_HEADS, 1, tm), lambda b, t: (b, 0, 0, t)),
        ],
        out_shape=[
            jax.ShapeDtypeStruct((BATCH, N_HEADS, SEQ, LANES), BF16),
            jax.ShapeDtypeStruct((BATCH, N_HEADS, SEQ, LANES), BF16),
            jax.ShapeDtypeStruct((BATCH, N_HEADS, N_KT, V_ROWS, ATT_T), BF16),
            jax.ShapeDtypeStruct((BATCH, D_MODEL, SEQ), BF16),
            jax.ShapeDtypeStruct((BATCH, N_HEADS, 1, SEQ), F32),
        ],
        scratch_shapes=[pltpu.VMEM((1, LANES), F32)],
        compiler_params=_cparams(("arbitrary", "arbitrary")),
        name="fox_proj",
    )(x1, mkv, mb, gkv, gb, wk, wvt, wf, fb, wq, wzt, kng, qng, part)


def _attn_kernel(bounded_ref, q_ref, k_ref, v_ref, fq_ref, szt_ref, x_ref, mb_ref, w_ref, o_ref,
                 acc_ref, m_ref, ot_ref, s_ref):
    t = ATT_T
    qi = pl.program_id(1)
    trans_b = (((1,), (1,)), ((), ()))
    ki = lax.broadcasted_iota(jnp.int32, (t, t), 0)
    qq = lax.broadcasted_iota(jnp.int32, (t, t), 1)
    visible = ki <= qq
    heads = [(h, h) for h in range(N_HEADS)]

    def qk(h, kj, diagonal):
        k = k_ref[h, pl.ds(pl.multiple_of(kj * t, t), t), :]
        s = lax.dot_general(k, q_ref[h], trans_b, preferred_element_type=F32)
        return jnp.where(visible, s, -jnp.inf) if diagonal else s

    def finish():
        for i, h in heads:
            a = acc_ref[i]
            ot_ref[h * HEAD_DIM:(h + 1) * HEAD_DIM, :] = a[0:HEAD_DIM] * (1.0 / a[HEAD_DIM:HEAD_DIM + 1])

    def bounded_blocks(kjs, diagonal):
        items = [(n * N_HEADS + i, i, h, kj) for n, kj in enumerate(kjs) for i, h in heads]
        for slot, _i, h, kj in items:
            s_ref[slot] = qk(h, kj, diagonal)
        for slot, i, h, kj in items:
            p = jnp.exp2(s_ref[slot] + fq_ref[h]).astype(BF16)
            acc_ref[i] += jnp.dot(v_ref[h, kj], p, preferred_element_type=F32)

    def attend_bounded():
        def pair_body(j, _):
            bounded_blocks([2 * j, 2 * j + 1], False)
            return 0

        lax.fori_loop(0, qi // 2, pair_body, 0)

        @pl.when(qi % 2 == 1)
        def _():
            bounded_blocks([qi - 1], False)

        bounded_blocks([qi], True)
        finish()

    def running_max_blocks(kj, diagonal):
        s_max = []
        for i, h in heads:
            s = qk(h, kj, diagonal)
            s_ref[i] = s
            s_max.append(jnp.max(s, axis=0, keepdims=True))
        for (i, h), sm in zip(heads, s_max):
            fq = fq_ref[h]
            m_old = m_ref[i]
            m_new = jnp.maximum(m_old, sm + fq)
            alpha = jnp.exp2(m_old - m_new)
            p = jnp.exp2(s_ref[i] + (fq - m_new)).astype(BF16)
            acc_ref[i] = alpha * acc_ref[i] + jnp.dot(v_ref[h, kj], p, preferred_element_type=F32)
            m_ref[i] = m_new

    def attend_running_max():
        def k_body(kj, _):
            running_max_blocks(kj, False)
            return 0

        lax.fori_loop(0, qi, k_body, 0)
        running_max_blocks(qi, True)
        finish()

    for i, _h in heads:
        acc_ref[i] = jnp.zeros((V_ROWS, t), F32)

    @pl.when(bounded_ref[0] == 1)
    def _():
        attend_bounded()

    @pl.when(bounded_ref[0] != 1)
    def _():
        for i, _h in heads:
            m_ref[i] = jnp.full((1, t), -jnp.inf, F32)
        attend_running_max()

    yt = ot_ref[...] * szt_ref[...].astype(F32)
    y = yt.T.astype(BF16)
    out = jnp.dot(y, w_ref[...], preferred_element_type=F32)
    gate = mb_ref[...][:, 2 * D_MODEL:]
    o_ref[...] = x_ref[...] + gate * out


def _fox_attn(bounded, qaug, kaug, vt, ft, szt, x1, mb, w_out):
    t = ATT_T
    return pl.pallas_call(
        _attn_kernel,
        grid=(BATCH, SEQ // t),
        in_specs=[
            pl.BlockSpec(memory_space=pltpu.SMEM),
            pl.BlockSpec((None, N_HEADS, t, LANES), lambda b, i: (b, 0, i, 0)),
            pl.BlockSpec((None, N_HEADS, SEQ, LANES), lambda b, i: (b, 0, 0, 0)),
            pl.BlockSpec((None, N_HEADS, N_KT, V_ROWS, t), lambda b, i: (b, 0, 0, 0, 0)),
            pl.BlockSpec((None, N_HEADS, 1, t), lambda b, i: (b, 0, 0, i)),
            pl.BlockSpec((None, D_MODEL, t), lambda b, i: (b, 0, i)),
            pl.BlockSpec((None, t, D_MODEL), lambda b, i: (b, i, 0)),
            pl.BlockSpec((None, 1, 3 * D_MODEL), lambda b, i: (b, 0, 0)),
            _const_spec((D_MODEL, D_MODEL)),
        ],
        out_specs=pl.BlockSpec((None, t, D_MODEL), lambda b, i: (b, i, 0)),
        out_shape=jax.ShapeDtypeStruct((BATCH, SEQ, D_MODEL), F32),
        scratch_shapes=[
            pltpu.VMEM((N_HEADS, V_ROWS, t), F32),
            pltpu.VMEM((N_HEADS, 1, t), F32),
            pltpu.VMEM((D_MODEL, t), F32),
            pltpu.VMEM((2 * N_HEADS, t, t), F32),
        ],
        compiler_params=_cparams(("arbitrary", "arbitrary")),
        name="fox_attn",
    )(bounded, qaug, kaug, vt, ft, szt, x1, mb, w_out)


def kernel(x, c, a_norm_g, a_mod_w, a_mod_b, a_w_in, a_log_dt, a_A_re, a_A_im, a_B_re, a_B_im,
           a_C_re, a_C_im, a_D, a_w_glu, a_b_glu, a_w_out, kv_norm_g, kv_mod_w, kv_mod_b, kv_w,
           kv_f_bias, k_norm_g, b_norm_g, b_mod_w, b_mod_b, b_w_in, q_norm_g, b_w_out):
    assert x.shape == (BATCH, SEQ, D_MODEL) and a_mod_w.shape[0] == 1 and b_mod_w.shape[0] == 1
    aw = N_HEADS * HEAD_DIM

    mod_a = _modulation(c, a_mod_w[0], a_mod_b[0])
    mod_kv = _modulation(c, kv_mod_w, kv_mod_b)
    mod_b = _modulation(c, b_mod_w[0], b_mod_b[0])

    w2, wc, cb, a2r, a2i = _s5_params(a_log_dt[0], a_A_re[0], a_A_im[0], a_B_re[0], a_B_im[0],
                                      a_C_re[0], a_C_im[0])
    x1 = _s5_layer(x, mod_a, a_norm_g[0].reshape(1, D_MODEL), a_w_in[0].astype(BF16), w2, a2r, a2i,
                   wc, cb, a_D[0].reshape(1, D_MODEL), a_w_glu[0].astype(BF16),
                   a_b_glu[0].reshape(1, D_MODEL), a_w_out[0].astype(BF16))

    wk = kv_w[:, :aw].astype(BF16)
    wvt = kv_w[:, aw:2 * aw].T.astype(BF16)
    wf = jnp.pad(jnp.repeat(kv_w[:, 2 * aw:], 3, axis=1),
                 ((0, 0), (0, LANES - 3 * N_HEADS))).astype(BF16)
    fb = jnp.pad(jnp.repeat(kv_f_bias, 3), (0, LANES - 3 * N_HEADS)).reshape(1, LANES)
    part = (jnp.arange(LANES, dtype=jnp.int32) % 3).reshape(1, LANES)
    wq = b_w_in[0][:, :aw].astype(BF16)
    wzt = b_w_in[0][:, aw:].T.astype(BF16)
    kng = jnp.tile(k_norm_g, 2).reshape(1, LANES)
    qng = (jnp.tile(q_norm_g[0], 2) * (HEAD_DIM ** -0.5 * LOG2E)).reshape(1, LANES)
    mkv3 = mod_kv.reshape(BATCH, 1, 2 * D_MODEL)
    mb3 = mod_b.reshape(BATCH, 1, 3 * D_MODEL)
    kaug, qaug, vt, szt, ft = _fox_proj(x1, mkv3, mb3, kv_norm_g.reshape(1, D_MODEL),
                                        b_norm_g[0].reshape(1, D_MODEL), wk, wvt, wf, fb,
                                        wq, wzt, kng, qng, part)
    qk_bound = HEAD_DIM * jnp.max(jnp.abs(kng)) * jnp.max(jnp.abs(qng))
    bounded = (qk_bound <= MAX_UNSTABILISED_LOG2).astype(jnp.int32).reshape(1)
    return _fox_attn(bounded, qaug, kaug, vt, ft, szt, x1, mb3, b_w_out[0].astype(BF16))
```

```python
import jax
import jax.numpy as jnp
from jax import lax
from jax.experimental import pallas as pl
from jax.experimental.pallas import tpu as pltpu

D_MODEL = 1024
BATCH = 8
SEQ = 2048
GROUP = 16
N_GROUPS = D_MODEL // GROUP
STATE = 64
N_STATES = N_GROUPS * STATE
N_HEADS = 16
HEAD_DIM = 64
EPS = 1e-6

F32 = jnp.float32
BF16 = jnp.bfloat16

SUBLANES = 8
LANES = 128
MXU_DIM = 256
VMEM_LIMIT_BYTES = 56 * 1024 * 1024

S5_T = 64
S5_PAIRS = S5_T // 2
S5_CH = LANES
N_CH = D_MODEL // S5_CH
CH_GROUPS = S5_CH // GROUP
CH_STATES = CH_GROUPS * STATE
PROJ_TM = 512
ATT_T = 256
N_KT = SEQ // ATT_T
KT_PER_PROJ = PROJ_TM // ATT_T
MAX_UNSTABILISED_LOG2 = 64.0
V_ROWS = HEAD_DIM + 16
BIAS_LANE = HEAD_DIM
MOD_BN = 512
LOG2E = 1.4426950408889634


def _cparams(sem):
    return pltpu.CompilerParams(dimension_semantics=sem, vmem_limit_bytes=VMEM_LIMIT_BYTES)


def _const_spec(shape):
    nd = len(shape)
    return pl.BlockSpec(shape, lambda *_: (0,) * nd, pipeline_mode=pl.Buffered(1))


def _split3(x):
    hi = x.astype(BF16).astype(F32)
    r = x - hi
    mid = r.astype(BF16).astype(F32)
    lo = (r - mid).astype(BF16).astype(F32)
    return hi, mid, lo


def _mod_kernel(c_ref, w_ref, b_ref, o_ref):
    c = c_ref[...]
    s = (c * jax.nn.sigmoid(c)).astype(BF16)
    o_ref[...] = jnp.dot(s, w_ref[...].astype(BF16), preferred_element_type=F32) + b_ref[...]


def _modulation(c, w, b):
    n = w.shape[1]
    return pl.pallas_call(
        _mod_kernel,
        grid=(n // MOD_BN,),
        in_specs=[
            pl.BlockSpec((BATCH, D_MODEL), lambda j: (0, 0)),
            pl.BlockSpec((D_MODEL, MOD_BN), lambda j: (0, j)),
            pl.BlockSpec((1, MOD_BN), lambda j: (0, j)),
        ],
        out_specs=pl.BlockSpec((BATCH, MOD_BN), lambda j: (0, j)),
        out_shape=jax.ShapeDtypeStruct((BATCH, n), F32),
        compiler_params=_cparams(("arbitrary",)),
        name="modulation",
    )(c, w, b.reshape(1, n))


def _s5_kernel(x_ref, mod_ref, g_ref, win_ref, w2_ref, a2r_ref, a2i_ref, wc_ref, cb_ref, d_ref,
               wglu_ref, bglu_ref, wout_ref, o_ref, sre, sim, st_re, st_im):
    tm = S5_T * BATCH
    mh = S5_PAIRS * BATCH

    @pl.when(pl.program_id(0) == 0)
    def _():
        st_re[...] = jnp.zeros_like(st_re)
        st_im[...] = jnp.zeros_like(st_im)

    x4 = jnp.swapaxes(x_ref[...], 0, 1).reshape(S5_PAIRS, 2, BATCH, D_MODEL)
    x3 = jnp.concatenate([x4[:, 0], x4[:, 1]], axis=0)
    mod = mod_ref[...]
    shift = mod[:, :D_MODEL]
    scale = mod[:, D_MODEL:2 * D_MODEL]
    gate = mod[:, 2 * D_MODEL:]
    ms = jnp.mean(x3 * x3, axis=-1, keepdims=True)
    h3 = (x3 * lax.rsqrt(ms + EPS)) * (g_ref[...] * (1.0 + scale))[None] + shift[None]
    h = h3.reshape(tm, D_MODEL).astype(BF16)
    uz = jnp.dot(h, win_ref[...], preferred_element_type=F32)
    u = uz[:, :D_MODEL]
    z = uz[:, D_MODEL:]
    ub = u.astype(BF16)
    ue = ub[:mh]
    uo = ub[mh:]

    for c in range(N_CH):
        cols = slice(c * S5_CH, (c + 1) * S5_CH)
        lhs = jnp.concatenate([ue[:, cols], uo[:, cols]], axis=1)
        p = jnp.dot(lhs, w2_ref[c], preferred_element_type=F32)
        sre[c, BATCH:, :] = p[:, :CH_STATES]
        sim[c, BATCH:, :] = p[:, CH_STATES:]

    for c in range(N_CH):
        a2r = a2r_ref[c]
        a2i = a2i_ref[c]
        sr = st_re[c]
        si = st_im[c]
        sre[c, 0:BATCH, :] = sr
        sim[c, 0:BATCH, :] = si
        for m in range(S5_PAIRS):
            rows = pl.ds((m + 1) * BATCH, BATCH)
            sr, si = (a2r * sr - a2i * si + sre[c, rows, :],
                      a2r * si + a2i * sr + sim[c, rows, :])
            sre[c, rows, :] = sr
            sim[c, rows, :] = si
        st_re[c] = sr
        st_im[c] = si

    y_even, y_odd = [], []
    for c in range(N_CH):
        res = (jnp.dot(sre[c].astype(BF16), wc_ref[c, :CH_STATES, :], preferred_element_type=F32)
               + jnp.dot(sim[c].astype(BF16), wc_ref[c, CH_STATES:, :], preferred_element_type=F32))
        direct = jnp.dot(ue[:, c * S5_CH:(c + 1) * S5_CH], cb_ref[c], preferred_element_type=F32)
        y_odd.append(res[BATCH:, :S5_CH])
        y_even.append(res[:mh, S5_CH:] + direct)
    y = jnp.concatenate([jnp.concatenate(y_even, axis=1), jnp.concatenate(y_odd, axis=1)], axis=0)
    y = y + d_ref[...] * u
    y = jax.nn.gelu(y)
    gl = jnp.dot(y.astype(BF16), wglu_ref[...], preferred_element_type=F32) + bglu_ref[...]
    y = y * jax.nn.sigmoid(gl)
    y = y * (z * jax.nn.sigmoid(z))
    o = jnp.dot(y.astype(BF16), wout_ref[...], preferred_element_type=F32)
    out3 = x3 + gate[None] * o.reshape(S5_T, BATCH, D_MODEL)
    out3 = jnp.stack([out3[:S5_PAIRS], out3[S5_PAIRS:]], axis=1).reshape(S5_T, BATCH, D_MODEL)
    o_ref[...] = jnp.swapaxes(out3, 0, 1)


def _s5_layer(x, mod, g, w_in, w2, a2r, a2i, wc, cb, dvec, w_glu, b_glu, w_out):
    buf_rows = (S5_PAIRS + 1) * BATCH
    return pl.pallas_call(
        _s5_kernel,
        grid=(SEQ // S5_T,),
        in_specs=[
            pl.BlockSpec((BATCH, S5_T, D_MODEL), lambda i: (0, i, 0)),
            _const_spec((BATCH, 3 * D_MODEL)),
            _const_spec((1, D_MODEL)),
            _const_spec((D_MODEL, 2 * D_MODEL)),
            _const_spec((N_CH, 2 * S5_CH, 2 * CH_STATES)),
            _const_spec((N_CH, BATCH, CH_STATES)),
            _const_spec((N_CH, BATCH, CH_STATES)),
            _const_spec((N_CH, 2 * CH_STATES, 2 * S5_CH)),
            _const_spec((N_CH, S5_CH, S5_CH)),
            _const_spec((1, D_MODEL)),
            _const_spec((D_MODEL, D_MODEL)),
            _const_spec((1, D_MODEL)),
            _const_spec((D_MODEL, D_MODEL)),
        ],
        out_specs=pl.BlockSpec((BATCH, S5_T, D_MODEL), lambda i: (0, i, 0)),
        out_shape=jax.ShapeDtypeStruct((BATCH, SEQ, D_MODEL), F32),
        scratch_shapes=[
            pltpu.VMEM((N_CH, buf_rows, CH_STATES), F32),
            pltpu.VMEM((N_CH, buf_rows, CH_STATES), F32),
            pltpu.VMEM((N_CH, BATCH, CH_STATES), F32),
            pltpu.VMEM((N_CH, BATCH, CH_STATES), F32),
        ],
        compiler_params=_cparams(("arbitrary",)),
        name="s5_layer",
    )(x, mod, g, w_in, w2, a2r, a2i, wc, cb, dvec, w_glu, b_glu, w_out)


def _s5_params(log_dt, a_re, a_im, b_re, b_im, c_re, c_im):
    dt = jnp.exp(log_dt)[:, None]
    mag = jnp.exp(a_re * dt)
    ar, ai = mag * jnp.cos(a_im * dt), mag * jnp.sin(a_im * dt)
    den = a_re * a_re + a_im * a_im
    nr = ar - 1.0
    coef_r = (nr * a_re + ai * a_im) / den
    coef_i = (ai * a_re - nr * a_im) / den
    bb_r = coef_r[..., None] * b_re - coef_i[..., None] * b_im
    bb_i = coef_r[..., None] * b_im + coef_i[..., None] * b_re
    a2r, a2i = ar * ar - ai * ai, 2.0 * ar * ai
    ab_r = ar[..., None] * bb_r - ai[..., None] * bb_i
    ab_i = ar[..., None] * bb_i + ai[..., None] * bb_r
    ca_r = c_re * ar[:, None, :] - c_im * ai[:, None, :]
    ca_i = c_re * ai[:, None, :] + c_im * ar[:, None, :]
    hi = lax.Precision.HIGHEST
    cb = (jnp.einsum('gcp,gpk->gck', c_re, bb_r, precision=hi)
          - jnp.einsum('gcp,gpk->gck', c_im, bb_i, precision=hi))

    def block_diag(t):
        r, q = t.shape[1], t.shape[2]
        t = t.reshape(N_CH, CH_GROUPS * r, q)
        rows_blk = jnp.arange(CH_GROUPS * r) // r
        cols_blk = jnp.arange(CH_GROUPS * q) // q
        return jnp.where(rows_blk[:, None] == cols_blk[None, :], jnp.tile(t, (1, 1, CH_GROUPS)), 0.0)

    def in_rows(b_r, b_i):
        return jnp.concatenate([block_diag(b_r.transpose(0, 2, 1)), block_diag(b_i.transpose(0, 2, 1))],
                               axis=2)

    def out_cols(c_r, c_i):
        return jnp.concatenate([block_diag(c_r.transpose(0, 2, 1)), block_diag(-c_i.transpose(0, 2, 1))],
                               axis=1)

    w2 = jnp.concatenate([in_rows(ab_r, ab_i), in_rows(bb_r, bb_i)], axis=1).astype(BF16)
    wc = jnp.concatenate([out_cols(c_re, c_im), out_cols(ca_r, ca_i)], axis=2).astype(BF16)
    cbp = block_diag(cb.transpose(0, 2, 1)).astype(BF16)

    def rows(a):
        return jnp.broadcast_to(a.reshape(N_CH, 1, CH_STATES), (N_CH, BATCH, CH_STATES))

    return w2, wc, cbp, rows(a2r), rows(a2i)


def _log_sigmoid(x):
    return jnp.minimum(x, 0.0) - jnp.log1p(jnp.exp(-jnp.abs(x)))


def _proj_kernel(x_ref, mkv_ref, mb_ref, gkv_ref, gb_ref, wk_ref, wvt_ref, wf_ref, fb_ref,
                 wq_ref, wzt_ref, kng_ref, qng_ref, part_ref,
                 kaug_ref, qaug_ref, vt_ref, szt_ref, ft_ref, carry_ref):
    tm = PROJ_TM

    @pl.when(pl.program_id(1) == 0)
    def _():
        carry_ref[...] = jnp.zeros_like(carry_ref)

    x = x_ref[...]
    xn = x * lax.rsqrt(jnp.mean(x * x, axis=-1, keepdims=True) + EPS)
    mkv = mkv_ref[...]
    h2 = xn * (gkv_ref[...] * (1.0 + mkv[:, D_MODEL:])) + mkv[:, :D_MODEL]
    mb = mb_ref[...]
    h3 = xn * (gb_ref[...] * (1.0 + mb[:, D_MODEL:2 * D_MODEL])) + mb[:, :D_MODEL]
    h2b = h2.astype(BF16)
    h3b = h3.astype(BF16)
    trans_b = (((1,), (1,)), ((), ()))

    f = jnp.dot(h2b, wf_ref[...], preferred_element_type=F32) + fb_ref[...]
    ls = _log_sigmoid(f)
    ri = lax.broadcasted_iota(jnp.int32, (tm, tm), 0)
    ci = lax.broadcasted_iota(jnp.int32, (tm, tm), 1)
    tri = jnp.where(ci <= ri, 1.0, 0.0).astype(BF16)
    l_hi = ls.astype(BF16)
    l_lo = (ls - l_hi.astype(F32)).astype(BF16)
    fcum = (jnp.dot(tri, l_hi, preferred_element_type=F32)
            + jnp.dot(tri, l_lo, preferred_element_type=F32)) + carry_ref[...]
    carry_ref[...] = fcum[tm - 1:tm, :]

    f2 = fcum * LOG2E
    fct = f2.T
    for h in range(N_HEADS):
        ft_ref[h] = fct[3 * h:3 * h + 1, :]
    n_hi, n_mid, n_lo = _split3(-f2)
    part = part_ref[...]
    f_parts = jnp.where(part == 0, n_hi, jnp.where(part == 1, n_mid, n_lo))

    lane = lax.broadcasted_iota(jnp.int32, (tm, LANES), 1)
    low = lane < HEAD_DIM
    ones_cols = jnp.where(lane < BIAS_LANE + 3, 1.0, 0.0)
    kng = kng_ref[...]
    qng = qng_ref[...]

    def pair_scale(sq):
        ss_a = jnp.sum(jnp.where(low, sq, 0.0), axis=-1, keepdims=True)
        ss_b = jnp.sum(jnp.where(low, 0.0, sq), axis=-1, keepdims=True)
        return lax.rsqrt(jnp.where(low, ss_a, ss_b) * (1.0 / HEAD_DIM) + EPS)

    heads_per_chunk = MXU_DIM // HEAD_DIM
    pad_row = lax.broadcasted_iota(jnp.int32, (heads_per_chunk, V_ROWS - HEAD_DIM, ATT_T), 1)
    ones_rows = jnp.where(pad_row == 0, 1.0, 0.0).astype(BF16)
    for c in range(D_MODEL // MXU_DIM):
        cols = slice(c * MXU_DIM, (c + 1) * MXU_DIM)
        hs = slice(c * heads_per_chunk, (c + 1) * heads_per_chunk)
        k = jnp.dot(h2b, wk_ref[:, cols], preferred_element_type=F32)
        q = jnp.dot(h3b, wq_ref[:, cols], preferred_element_type=F32)
        for lp in range(MXU_DIM // LANES):
            kp = k[:, lp * LANES:(lp + 1) * LANES]
            qp = q[:, lp * LANES:(lp + 1) * LANES]
            knp = (kp * pair_scale(kp * kp)) * kng
            qnp = (qp * pair_scale(qp * qp)) * qng
            for half in range(2):
                h = c * heads_per_chunk + 2 * lp + half
                ka = knp if half == 0 else pltpu.roll(knp, HEAD_DIM, 1)
                qa = qnp if half == 0 else pltpu.roll(qnp, HEAD_DIM, 1)
                bias = pltpu.roll(f_parts, BIAS_LANE - 3 * h, 1)
                bias = jnp.where(lane < BIAS_LANE + 3, bias, 0.0)
                kaug_ref[h] = jnp.where(low, ka, bias).astype(BF16)
                qaug_ref[h] = jnp.where(low, qa, ones_cols).astype(BF16)

        vt = lax.dot_general(wvt_ref[cols, :], h2b, trans_b, preferred_element_type=F32)
        vt3 = vt.reshape(heads_per_chunk, HEAD_DIM, tm).astype(BF16)
        for kt in range(KT_PER_PROJ):
            vt_ref[hs, kt, 0:HEAD_DIM, :] = vt3[:, :, kt * ATT_T:(kt + 1) * ATT_T]
            vt_ref[hs, kt, HEAD_DIM:, :] = ones_rows
        zt = lax.dot_general(wzt_ref[cols, :], h3b, trans_b, preferred_element_type=F32)
        szt_ref[cols, :] = (zt * jax.nn.sigmoid(zt)).astype(BF16)


def _fox_proj(x1, mkv, mb, gkv, gb, wk, wvt, wf, fb, wq, wzt, kng, qng, part):
    tm = PROJ_TM
    row = lambda b, t: (b, t, 0)
    per_b = lambda b, t: (b, 0, 0)
    return pl.pallas_call(
        _proj_kernel,
        grid=(BATCH, SEQ // tm),
        in_specs=[
            pl.BlockSpec((None, tm, D_MODEL), row),
            pl.BlockSpec((None, 1, 2 * D_MODEL), per_b),
            pl.BlockSpec((None, 1, 3 * D_MODEL), per_b),
            _const_spec((1, D_MODEL)),
            _const_spec((1, D_MODEL)),
            _const_spec((D_MODEL, D_MODEL)),
            _const_spec((D_MODEL, D_MODEL)),
            _const_spec((D_MODEL, LANES)),
            _const_spec((1, LANES)),
            _const_spec((D_MODEL, D_MODEL)),
            _const_spec((D_MODEL, D_MODEL)),
            _const_spec((1, LANES)),
            _const_spec((1, LANES)),
            _const_spec((1, LANES)),
        ],
        out_specs=[
            pl.BlockSpec((None, N_HEADS, tm, LANES), lambda b, t: (b, 0, t, 0)),
            pl.BlockSpec((None, N_HEADS, tm, LANES), lambda b, t: (b, 0, t, 0)),
            pl.BlockSpec((None, N_HEADS, KT_PER_PROJ, V_ROWS, ATT_T), lambda b, t: (b, 0, t, 0, 0)),
            pl.BlockSpec((None, D_MODEL, tm), lambda b, t: (b, 0, t)),
            pl.BlockSpec((None, N_HEADS, 1, tm), lambda b, t: (b, 0, 0, t)),
        ],
        out_shape=[
            jax.ShapeDtypeStruct((BATCH, N_HEADS, SEQ, LANES), BF16),
            jax.ShapeDtypeStruct((BATCH, N_HEADS, SEQ, LANES), BF16),
            jax.ShapeDtypeStruct((BATCH, N_HEADS, N_KT, V_ROWS, ATT_T), BF16),
            jax.ShapeDtypeStruct((BATCH, D_MODEL, SEQ), BF16),
            jax.ShapeDtypeStruct((BATCH, N_HEADS, 1, SEQ), F32),
        ],
        scratch_shapes=[pltpu.VMEM((1, LANES), F32)],
        compiler_params=_cparams(("arbitrary", "arbitrary")),
        name="fox_proj",
    )(x1, mkv, mb, gkv, gb, wk, wvt, wf, fb, wq, wzt, kng, qng, part)


def _attn_kernel(bounded_ref, q_ref, k_ref, v_ref, fq_ref, szt_ref, x_ref, mb_ref, w_ref, o_ref,
                 acc_ref, m_ref, ot_ref, s_ref, p_ref):
    t = ATT_T
    qi = pl.program_id(1)
    trans_b = (((1,), (1,)), ((), ()))
    ki = lax.broadcasted_iota(jnp.int32, (t, t), 0)
    qq = lax.broadcasted_iota(jnp.int32, (t, t), 1)
    visible = ki <= qq
    heads = [(h, h) for h in range(N_HEADS)]

    def qk(h, kj, diagonal):
        k = k_ref[h, pl.ds(pl.multiple_of(kj * t, t), t), :]
        s = lax.dot_general(k, q_ref[h], trans_b, preferred_element_type=F32)
        return jnp.where(visible, s, -jnp.inf) if diagonal else s

    def finish():
        for i, h in heads:
            a = acc_ref[i]
            ot_ref[h * HEAD_DIM:(h + 1) * HEAD_DIM, :] = a[0:HEAD_DIM] * (1.0 / a[HEAD_DIM:HEAD_DIM + 1])

    def bounded_blocks(kjs, diagonal):
        items = [(n * N_HEADS + i, i, h, kj) for n, kj in enumerate(kjs) for i, h in heads]
        for slot, _i, h, kj in items:
            p_ref[slot] = jnp.exp2(qk(h, kj, diagonal) + fq_ref[h]).astype(BF16)
        for slot, i, h, kj in items:
            acc_ref[i] += jnp.dot(v_ref[h, kj], p_ref[slot], preferred_element_type=F32)

    def attend_bounded():
        def pair_body(j, _):
            bounded_blocks([2 * j, 2 * j + 1], False)
            return 0

        lax.fori_loop(0, qi // 2, pair_body, 0)

        @pl.when(qi % 2 == 1)
        def _():
            bounded_blocks([qi - 1], False)

        bounded_blocks([qi], True)
        finish()

    def running_max_blocks(kj, diagonal):
        s_max = []
        for i, h in heads:
            s = qk(h, kj, diagonal)
            s_ref[i] = s
            s_max.append(jnp.max(s, axis=0, keepdims=True))
        for (i, h), sm in zip(heads, s_max):
            fq = fq_ref[h]
            m_old = m_ref[i]
            m_new = jnp.maximum(m_old, sm + fq)
            alpha = jnp.exp2(m_old - m_new)
            p = jnp.exp2(s_ref[i] + (fq - m_new)).astype(BF16)
            acc_ref[i] = alpha * acc_ref[i] + jnp.dot(v_ref[h, kj], p, preferred_element_type=F32)
            m_ref[i] = m_new

    def attend_running_max():
        def k_body(kj, _):
            running_max_blocks(kj, False)
            return 0

        lax.fori_loop(0, qi, k_body, 0)
        running_max_blocks(qi, True)
        finish()

    for i, _h in heads:
        acc_ref[i] = jnp.zeros((V_ROWS, t), F32)

    @pl.when(bounded_ref[0] == 1)
    def _():
        attend_bounded()

    @pl.when(bounded_ref[0] != 1)
    def _():
        for i, _h in heads:
            m_ref[i] = jnp.full((1, t), -jnp.inf, F32)
        attend_running_max()

    yt = ot_ref[...] * szt_ref[...].astype(F32)
    y = yt.T.astype(BF16)
    out = jnp.dot(y, w_ref[...], preferred_element_type=F32)
    gate = mb_ref[...][:, 2 * D_MODEL:]
    o_ref[...] = x_ref[...] + gate * out


def _fox_attn(bounded, qaug, kaug, vt, ft, szt, x1, mb, w_out):
    t = ATT_T
    return pl.pallas_call(
        _attn_kernel,
        grid=(BATCH, SEQ // t),
        in_specs=[
            pl.BlockSpec(memory_space=pltpu.SMEM),
            pl.BlockSpec((None, N_HEADS, t, LANES), lambda b, i: (b, 0, i, 0)),
            pl.BlockSpec((None, N_HEADS, SEQ, LANES), lambda b, i: (b, 0, 0, 0)),
            pl.BlockSpec((None, N_HEADS, N_KT, V_ROWS, t), lambda b, i: (b, 0, 0, 0, 0)),
            pl.BlockSpec((None, N_HEADS, 1, t), lambda b, i: (b, 0, 0, i)),
            pl.BlockSpec((None, D_MODEL, t), lambda b, i: (b, 0, i)),
            pl.BlockSpec((None, t, D_MODEL), lambda b, i: (b, i, 0)),
            pl.BlockSpec((None, 1, 3 * D_MODEL), lambda b, i: (b, 0, 0)),
            _const_spec((D_MODEL, D_MODEL)),
        ],
        out_specs=pl.BlockSpec((None, t, D_MODEL), lambda b, i: (b, i, 0)),
        out_shape=jax.ShapeDtypeStruct((BATCH, SEQ, D_MODEL), F32),
        scratch_shapes=[
            pltpu.VMEM((N_HEADS, V_ROWS, t), F32),
            pltpu.VMEM((N_HEADS, 1, t), F32),
            pltpu.VMEM((D_MODEL, t), F32),
            pltpu.VMEM((N_HEADS, t, t), F32),
            pltpu.VMEM((2 * N_HEADS, t, t), BF16),
        ],
        compiler_params=_cparams(("arbitrary", "arbitrary")),
        name="fox_attn",
    )(bounded, qaug, kaug, vt, ft, szt, x1, mb, w_out)


def kernel(x, c, a_norm_g, a_mod_w, a_mod_b, a_w_in, a_log_dt, a_A_re, a_A_im, a_B_re, a_B_im,
           a_C_re, a_C_im, a_D, a_w_glu, a_b_glu, a_w_out, kv_norm_g, kv_mod_w, kv_mod_b, kv_w,
           kv_f_bias, k_norm_g, b_norm_g, b_mod_w, b_mod_b, b_w_in, q_norm_g, b_w_out):
    assert x.shape == (BATCH, SEQ, D_MODEL) and a_mod_w.shape[0] == 1 and b_mod_w.shape[0] == 1
    aw = N_HEADS * HEAD_DIM

    mod_a = _modulation(c, a_mod_w[0], a_mod_b[0])
    mod_kv = _modulation(c, kv_mod_w, kv_mod_b)
    mod_b = _modulation(c, b_mod_w[0], b_mod_b[0])

    w2, wc, cb, a2r, a2i = _s5_params(a_log_dt[0], a_A_re[0], a_A_im[0], a_B_re[0], a_B_im[0],
                                      a_C_re[0], a_C_im[0])
    x1 = _s5_layer(x, mod_a, a_norm_g[0].reshape(1, D_MODEL), a_w_in[0].astype(BF16), w2, a2r, a2i,
                   wc, cb, a_D[0].reshape(1, D_MODEL), a_w_glu[0].astype(BF16),
                   a_b_glu[0].reshape(1, D_MODEL), a_w_out[0].astype(BF16))

    wk = kv_w[:, :aw].astype(BF16)
    wvt = kv_w[:, aw:2 * aw].T.astype(BF16)
    wf = jnp.pad(jnp.repeat(kv_w[:, 2 * aw:], 3, axis=1),
                 ((0, 0), (0, LANES - 3 * N_HEADS))).astype(BF16)
    fb = jnp.pad(jnp.repeat(kv_f_bias, 3), (0, LANES - 3 * N_HEADS)).reshape(1, LANES)
    part = (jnp.arange(LANES, dtype=jnp.int32) % 3).reshape(1, LANES)
    wq = b_w_in[0][:, :aw].astype(BF16)
    wzt = b_w_in[0][:, aw:].T.astype(BF16)
    kng = jnp.tile(k_norm_g, 2).reshape(1, LANES)
    qng = (jnp.tile(q_norm_g[0], 2) * (HEAD_DIM ** -0.5 * LOG2E)).reshape(1, LANES)
    mkv3 = mod_kv.reshape(BATCH, 1, 2 * D_MODEL)
    mb3 = mod_b.reshape(BATCH, 1, 3 * D_MODEL)
    kaug, qaug, vt, szt, ft = _fox_proj(x1, mkv3, mb3, kv_norm_g.reshape(1, D_MODEL),
                                        b_norm_g[0].reshape(1, D_MODEL), wk, wvt, wf, fb,
                                        wq, wzt, kng, qng, part)
    qk_bound = HEAD_DIM * jnp.max(jnp.abs(kng)) * jnp.max(jnp.abs(qng))
    bounded = (qk_bound <= MAX_UNSTABILISED_LOG2).astype(jnp.int32).reshape(1)
    return _fox_attn(bounded, qaug, kaug, vt, ft, szt, x1, mb3, b_w_out[0].astype(BF16))
```

```python
import jax
import jax.numpy as jnp
from jax import lax
from jax.experimental import pallas as pl
from jax.experimental.pallas import tpu as pltpu

D_MODEL = 1024
BATCH = 8
SEQ = 2048
GROUP = 16
N_GROUPS = D_MODEL // GROUP
STATE = 64
N_STATES = N_GROUPS * STATE
N_HEADS = 16
HEAD_DIM = 64
EPS = 1e-6

F32 = jnp.float32
BF16 = jnp.bfloat16

SUBLANES = 8
LANES = 128
MXU_DIM = 256
VMEM_LIMIT_BYTES = 56 * 1024 * 1024

S5_T = 64
S5_PAIRS = S5_T // 2
S5_CH = LANES
N_CH = D_MODEL // S5_CH
CH_GROUPS = S5_CH // GROUP
CH_STATES = CH_GROUPS * STATE
PROJ_TM = 512
ATT_T = 256
N_KT = SEQ // ATT_T
KT_PER_PROJ = PROJ_TM // ATT_T
MAX_UNSTABILISED_LOG2 = 64.0
V_ROWS = HEAD_DIM + 16
BIAS_LANE = HEAD_DIM
MOD_BN = 512
MOD_KV_BLOCKS = 2 * D_MODEL // MOD_BN
LOG2E = 1.4426950408889634


def _cparams(sem):
    return pltpu.CompilerParams(dimension_semantics=sem, vmem_limit_bytes=VMEM_LIMIT_BYTES)


def _const_spec(shape):
    nd = len(shape)
    return pl.BlockSpec(shape, lambda *_: (0,) * nd, pipeline_mode=pl.Buffered(1))


def _split3(x):
    hi = x.astype(BF16).astype(F32)
    r = x - hi
    mid = r.astype(BF16).astype(F32)
    lo = (r - mid).astype(BF16).astype(F32)
    return hi, mid, lo


def _mod_kernel(c_ref, wa_ref, ba_ref, wkv_ref, bkv_ref, wb_ref, bb_ref, oa_ref, okv_ref, ob_ref):
    c = c_ref[...]
    s = (c * jax.nn.sigmoid(c)).astype(BF16)

    def site(w_ref, b_ref, o_ref):
        o_ref[...] = jnp.dot(s, w_ref[...].astype(BF16), preferred_element_type=F32) + b_ref[...]

    site(wa_ref, ba_ref, oa_ref)
    site(wb_ref, bb_ref, ob_ref)

    @pl.when(pl.program_id(0) < MOD_KV_BLOCKS)
    def _():
        site(wkv_ref, bkv_ref, okv_ref)


def _modulation(c, wa, ba, wkv, bkv, wb, bb):
    col = lambda j: (0, j)
    col_kv = lambda j: (0, jnp.minimum(j, MOD_KV_BLOCKS - 1))
    n3, n2 = 3 * D_MODEL, 2 * D_MODEL
    return pl.pallas_call(
        _mod_kernel,
        grid=(n3 // MOD_BN,),
        in_specs=[
            pl.BlockSpec((BATCH, D_MODEL), lambda j: (0, 0)),
            pl.BlockSpec((D_MODEL, MOD_BN), col),
            pl.BlockSpec((1, MOD_BN), col),
            pl.BlockSpec((D_MODEL, MOD_BN), col_kv),
            pl.BlockSpec((1, MOD_BN), col_kv),
            pl.BlockSpec((D_MODEL, MOD_BN), col),
            pl.BlockSpec((1, MOD_BN), col),
        ],
        out_specs=[
            pl.BlockSpec((BATCH, MOD_BN), col),
            pl.BlockSpec((BATCH, MOD_BN), col_kv),
            pl.BlockSpec((BATCH, MOD_BN), col),
        ],
        out_shape=[
            jax.ShapeDtypeStruct((BATCH, n3), F32),
            jax.ShapeDtypeStruct((BATCH, n2), F32),
            jax.ShapeDtypeStruct((BATCH, n3), F32),
        ],
        compiler_params=_cparams(("arbitrary",)),
        name="modulation",
    )(c, wa, ba.reshape(1, n3), wkv, bkv.reshape(1, n2), wb, bb.reshape(1, n3))


def _s5_kernel(x_ref, mod_ref, g_ref, win_ref, w2_ref, a2r_ref, a2i_ref, wc_ref, cb_ref, d_ref,
               wglu_ref, bglu_ref, wout_ref, o_ref, sre, sim, st_re, st_im):
    tm = S5_T * BATCH
    mh = S5_PAIRS * BATCH

    @pl.when(pl.program_id(0) == 0)
    def _():
        st_re[...] = jnp.zeros_like(st_re)
        st_im[...] = jnp.zeros_like(st_im)

    x4 = jnp.swapaxes(x_ref[...], 0, 1).reshape(S5_PAIRS, 2, BATCH, D_MODEL)
    x3 = jnp.concatenate([x4[:, 0], x4[:, 1]], axis=0)
    mod = mod_ref[...]
    shift = mod[:, :D_MODEL]
    scale = mod[:, D_MODEL:2 * D_MODEL]
    gate = mod[:, 2 * D_MODEL:]
    ms = jnp.mean(x3 * x3, axis=-1, keepdims=True)
    h3 = (x3 * lax.rsqrt(ms + EPS)) * (g_ref[...] * (1.0 + scale))[None] + shift[None]
    h = h3.reshape(tm, D_MODEL).astype(BF16)
    uz = jnp.dot(h, win_ref[...], preferred_element_type=F32)
    u = uz[:, :D_MODEL]
    z = uz[:, D_MODEL:]
    ub = u.astype(BF16)
    ue = ub[:mh]
    uo = ub[mh:]

    for c in range(N_CH):
        cols = slice(c * S5_CH, (c + 1) * S5_CH)
        lhs = jnp.concatenate([ue[:, cols], uo[:, cols]], axis=1)
        p = jnp.dot(lhs, w2_ref[c], preferred_element_type=F32)
        sre[c, BATCH:, :] = p[:, :CH_STATES]
        sim[c, BATCH:, :] = p[:, CH_STATES:]

    for c in range(N_CH):
        a2r = a2r_ref[c]
        a2i = a2i_ref[c]
        sr = st_re[c]
        si = st_im[c]
        sre[c, 0:BATCH, :] = sr
        sim[c, 0:BATCH, :] = si
        for m in range(S5_PAIRS):
            rows = pl.ds((m + 1) * BATCH, BATCH)
            sr, si = (a2r * sr - a2i * si + sre[c, rows, :],
                      a2r * si + a2i * sr + sim[c, rows, :])
            sre[c, rows, :] = sr
            sim[c, rows, :] = si
        st_re[c] = sr
        st_im[c] = si

    y_even, y_odd = [], []
    for c in range(N_CH):
        res = (jnp.dot(sre[c].astype(BF16), wc_ref[c, :CH_STATES, :], preferred_element_type=F32)
               + jnp.dot(sim[c].astype(BF16), wc_ref[c, CH_STATES:, :], preferred_element_type=F32))
        direct = jnp.dot(ue[:, c * S5_CH:(c + 1) * S5_CH], cb_ref[c], preferred_element_type=F32)
        y_odd.append(res[BATCH:, :S5_CH])
        y_even.append(res[:mh, S5_CH:] + direct)
    y = jnp.concatenate([jnp.concatenate(y_even, axis=1), jnp.concatenate(y_odd, axis=1)], axis=0)
    y = y + d_ref[...] * u
    y = jax.nn.gelu(y)
    gl = jnp.dot(y.astype(BF16), wglu_ref[...], preferred_element_type=F32) + bglu_ref[...]
    y = y * jax.nn.sigmoid(gl)
    y = y * (z * jax.nn.sigmoid(z))
    o = jnp.dot(y.astype(BF16), wout_ref[...], preferred_element_type=F32)
    out3 = x3 + gate[None] * o.reshape(S5_T, BATCH, D_MODEL)
    out3 = jnp.stack([out3[:S5_PAIRS], out3[S5_PAIRS:]], axis=1).reshape(S5_T, BATCH, D_MODEL)
    o_ref[...] = jnp.swapaxes(out3, 0, 1)


def _s5_layer(x, mod, g, w_in, w2, a2r, a2i, wc, cb, dvec, w_glu, b_glu, w_out):
    buf_rows = (S5_PAIRS + 1) * BATCH
    return pl.pallas_call(
        _s5_kernel,
        grid=(SEQ // S5_T,),
        in_specs=[
            pl.BlockSpec((BATCH, S5_T, D_MODEL), lambda i: (0, i, 0)),
            _const_spec((BATCH, 3 * D_MODEL)),
            _const_spec((1, D_MODEL)),
            _const_spec((D_MODEL, 2 * D_MODEL)),
            _const_spec((N_CH, 2 * S5_CH, 2 * CH_STATES)),
            _const_spec((N_CH, BATCH, CH_STATES)),
            _const_spec((N_CH, BATCH, CH_STATES)),
            _const_spec((N_CH, 2 * CH_STATES, 2 * S5_CH)),
            _const_spec((N_CH, S5_CH, S5_CH)),
            _const_spec((1, D_MODEL)),
            _const_spec((D_MODEL, D_MODEL)),
            _const_spec((1, D_MODEL)),
            _const_spec((D_MODEL, D_MODEL)),
        ],
        out_specs=pl.BlockSpec((BATCH, S5_T, D_MODEL), lambda i: (0, i, 0)),
        out_shape=jax.ShapeDtypeStruct((BATCH, SEQ, D_MODEL), F32),
        scratch_shapes=[
            pltpu.VMEM((N_CH, buf_rows, CH_STATES), F32),
            pltpu.VMEM((N_CH, buf_rows, CH_STATES), F32),
            pltpu.VMEM((N_CH, BATCH, CH_STATES), F32),
            pltpu.VMEM((N_CH, BATCH, CH_STATES), F32),
        ],
        compiler_params=_cparams(("arbitrary",)),
        name="s5_layer",
    )(x, mod, g, w_in, w2, a2r, a2i, wc, cb, dvec, w_glu, b_glu, w_out)


def _s5_params(log_dt, a_re, a_im, b_re, b_im, c_re, c_im):
    dt = jnp.exp(log_dt)[:, None]
    mag = jnp.exp(a_re * dt)
    ar, ai = mag * jnp.cos(a_im * dt), mag * jnp.sin(a_im * dt)
    den = a_re * a_re + a_im * a_im
    nr = ar - 1.0
    coef_r = (nr * a_re + ai * a_im) / den
    coef_i = (ai * a_re - nr * a_im) / den
    bb_r = coef_r[..., None] * b_re - coef_i[..., None] * b_im
    bb_i = coef_r[..., None] * b_im + coef_i[..., None] * b_re
    a2r, a2i = ar * ar - ai * ai, 2.0 * ar * ai
    ab_r = ar[..., None] * bb_r - ai[..., None] * bb_i
    ab_i = ar[..., None] * bb_i + ai[..., None] * bb_r
    ca_r = c_re * ar[:, None, :] - c_im * ai[:, None, :]
    ca_i = c_re * ai[:, None, :] + c_im * ar[:, None, :]
    hi = lax.Precision.HIGHEST
    cb = (jnp.einsum('gcp,gpk->gck', c_re, bb_r, precision=hi)
          - jnp.einsum('gcp,gpk->gck', c_im, bb_i, precision=hi))

    def block_diag(t):
        r, q = t.shape[1], t.shape[2]
        t = t.reshape(N_CH, CH_GROUPS * r, q)
        rows_blk = jnp.arange(CH_GROUPS * r) // r
        cols_blk = jnp.arange(CH_GROUPS * q) // q
        return jnp.where(rows_blk[:, None] == cols_blk[None, :], jnp.tile(t, (1, 1, CH_GROUPS)), 0.0)

    def in_rows(b_r, b_i):
        return jnp.concatenate([block_diag(b_r.transpose(0, 2, 1)), block_diag(b_i.transpose(0, 2, 1))],
                               axis=2)

    def out_cols(c_r, c_i):
        return jnp.concatenate([block_diag(c_r.transpose(0, 2, 1)), block_diag(-c_i.transpose(0, 2, 1))],
                               axis=1)

    w2 = jnp.concatenate([in_rows(ab_r, ab_i), in_rows(bb_r, bb_i)], axis=1).astype(BF16)
    wc = jnp.concatenate([out_cols(c_re, c_im), out_cols(ca_r, ca_i)], axis=2).astype(BF16)
    cbp = block_diag(cb.transpose(0, 2, 1)).astype(BF16)

    def rows(a):
        return jnp.broadcast_to(a.reshape(N_CH, 1, CH_STATES), (N_CH, BATCH, CH_STATES))

    return w2, wc, cbp, rows(a2r), rows(a2i)


def _log_sigmoid(x):
    return jnp.minimum(x, 0.0) - jnp.log1p(jnp.exp(-jnp.abs(x)))


def _proj_kernel(x_ref, mkv_ref, mb_ref, gkv_ref, gb_ref, wk_ref, wvt_ref, wf_ref, fb_ref,
                 wq_ref, wzt_ref, kng_ref, qng_ref, part_ref,
                 kaug_ref, qaug_ref, vt_ref, szt_ref, ft_ref, carry_ref):
    tm = PROJ_TM

    @pl.when(pl.program_id(1) == 0)
    def _():
        carry_ref[...] = jnp.zeros_like(carry_ref)

    x = x_ref[...]
    xn = x * lax.rsqrt(jnp.mean(x * x, axis=-1, keepdims=True) + EPS)
    mkv = mkv_ref[...]
    h2 = xn * (gkv_ref[...] * (1.0 + mkv[:, D_MODEL:])) + mkv[:, :D_MODEL]
    mb = mb_ref[...]
    h3 = xn * (gb_ref[...] * (1.0 + mb[:, D_MODEL:2 * D_MODEL])) + mb[:, :D_MODEL]
    h2b = h2.astype(BF16)
    h3b = h3.astype(BF16)
    trans_b = (((1,), (1,)), ((), ()))

    f = jnp.dot(h2b, wf_ref[...], preferred_element_type=F32) + fb_ref[...]
    ls = _log_sigmoid(f)
    ri = lax.broadcasted_iota(jnp.int32, (tm, tm), 0)
    ci = lax.broadcasted_iota(jnp.int32, (tm, tm), 1)
    tri = jnp.where(ci <= ri, 1.0, 0.0).astype(BF16)
    l_hi = ls.astype(BF16)
    l_lo = (ls - l_hi.astype(F32)).astype(BF16)
    fcum = (jnp.dot(tri, l_hi, preferred_element_type=F32)
            + jnp.dot(tri, l_lo, preferred_element_type=F32)) + carry_ref[...]
    carry_ref[...] = fcum[tm - 1:tm, :]

    f2 = fcum * LOG2E
    fct = f2.T
    for h in range(N_HEADS):
        ft_ref[h] = fct[3 * h:3 * h + 1, :]
    n_hi, n_mid, n_lo = _split3(-f2)
    part = part_ref[...]
    f_parts = jnp.where(part == 0, n_hi, jnp.where(part == 1, n_mid, n_lo))

    lane = lax.broadcasted_iota(jnp.int32, (tm, LANES), 1)
    low = lane < HEAD_DIM
    ones_cols = jnp.where(lane < BIAS_LANE + 3, 1.0, 0.0)
    kng = kng_ref[...]
    qng = qng_ref[...]

    def pair_scale(sq):
        ss_a = jnp.sum(jnp.where(low, sq, 0.0), axis=-1, keepdims=True)
        ss_b = jnp.sum(jnp.where(low, 0.0, sq), axis=-1, keepdims=True)
        return lax.rsqrt(jnp.where(low, ss_a, ss_b) * (1.0 / HEAD_DIM) + EPS)

    heads_per_chunk = MXU_DIM // HEAD_DIM
    pad_row = lax.broadcasted_iota(jnp.int32, (heads_per_chunk, V_ROWS - HEAD_DIM, ATT_T), 1)
    ones_rows = jnp.where(pad_row == 0, 1.0, 0.0).astype(BF16)
    for c in range(D_MODEL // MXU_DIM):
        cols = slice(c * MXU_DIM, (c + 1) * MXU_DIM)
        hs = slice(c * heads_per_chunk, (c + 1) * heads_per_chunk)
        k = jnp.dot(h2b, wk_ref[:, cols], preferred_element_type=F32)
        q = jnp.dot(h3b, wq_ref[:, cols], preferred_element_type=F32)
        for lp in range(MXU_DIM // LANES):
            kp = k[:, lp * LANES:(lp + 1) * LANES]
            qp = q[:, lp * LANES:(lp + 1) * LANES]
            knp = (kp * pair_scale(kp * kp)) * kng
            qnp = (qp * pair_scale(qp * qp)) * qng
            for half in range(2):
                h = c * heads_per_chunk + 2 * lp + half
                ka = knp if half == 0 else pltpu.roll(knp, HEAD_DIM, 1)
                qa = qnp if half == 0 else pltpu.roll(qnp, HEAD_DIM, 1)
                bias = pltpu.roll(f_parts, BIAS_LANE - 3 * h, 1)
                bias = jnp.where(lane < BIAS_LANE + 3, bias, 0.0)
                kaug_ref[h] = jnp.where(low, ka, bias).astype(BF16)
                qaug_ref[h] = jnp.where(low, qa, ones_cols).astype(BF16)

        vt = lax.dot_general(wvt_ref[cols, :], h2b, trans_b, preferred_element_type=F32)
        vt3 = vt.reshape(heads_per_chunk, HEAD_DIM, tm).astype(BF16)
        for kt in range(KT_PER_PROJ):
            vt_ref[hs, kt, 0:HEAD_DIM, :] = vt3[:, :, kt * ATT_T:(kt + 1) * ATT_T]
            vt_ref[hs, kt, HEAD_DIM:, :] = ones_rows
        zt = lax.dot_general(wzt_ref[cols, :], h3b, trans_b, preferred_element_type=F32)
        szt_ref[cols, :] = (zt * jax.nn.sigmoid(zt)).astype(BF16)


def _fox_proj(x1, mkv, mb, gkv, gb, wk, wvt, wf, fb, wq, wzt, kng, qng, part):
    tm = PROJ_TM
    row = lambda b, t: (b, t, 0)
    per_b = lambda b, t: (b, 0, 0)
    return pl.pallas_call(
        _proj_kernel,
        grid=(BATCH, SEQ // tm),
        in_specs=[
            pl.BlockSpec((None, tm, D_MODEL), row),
            pl.BlockSpec((None, 1, 2 * D_MODEL), per_b),
            pl.BlockSpec((None, 1, 3 * D_MODEL), per_b),
            _const_spec((1, D_MODEL)),
            _const_spec((1, D_MODEL)),
            _const_spec((D_MODEL, D_MODEL)),
            _const_spec((D_MODEL, D_MODEL)),
            _const_spec((D_MODEL, LANES)),
            _const_spec((1, LANES)),
            _const_spec((D_MODEL, D_MODEL)),
            _const_spec((D_MODEL, D_MODEL)),
            _const_spec((1, LANES)),
            _const_spec((1, LANES)),
            _const_spec((1, LANES)),
        ],
        out_specs=[
            pl.BlockSpec((None, N_HEADS, tm, LANES), lambda b, t: (b, 0, t, 0)),
            pl.BlockSpec((None, N_HEADS, tm, LANES), lambda b, t: (b, 0, t, 0)),
            pl.BlockSpec((None, N_HEADS, KT_PER_PROJ, V_ROWS, ATT_T), lambda b, t: (b, 0, t, 0, 0)),
            pl.BlockSpec((None, D_MODEL, tm), lambda b, t: (b, 0, t)),
            pl.BlockSpec((None, N_HEADS, 1, tm), lambda b, t: (b, 0, 0, t)),
        ],
        out_shape=[
            jax.ShapeDtypeStruct((BATCH, N_HEADS, SEQ, LANES), BF16),
            jax.ShapeDtypeStruct((BATCH, N_HEADS, SEQ, LANES), BF16),
            jax.ShapeDtypeStruct((BATCH, N_HEADS, N_KT, V_ROWS, ATT_T), BF16),
            jax.ShapeDtypeStruct((BATCH, D_MODEL, SEQ), BF16),
            jax.ShapeDtypeStruct((BATCH, N_HEADS, 1, SEQ), F32),
        ],
        scratch_shapes=[pltpu.VMEM((1, LANES), F32)],
        compiler_params=_cparams(("arbitrary", "arbitrary")),
        name="fox_proj",
    )(x1, mkv, mb, gkv, gb, wk, wvt, wf, fb, wq, wzt, kng, qng, part)


def _attn_kernel(bounded_ref, q_ref, k_ref, v_ref, fq_ref, szt_ref, x_ref, mb_ref, w_ref, o_ref,
                 acc_ref, m_ref, ot_ref, s_ref, p_ref):
    t = ATT_T
    qi = pl.program_id(1)
    trans_b = (((1,), (1,)), ((), ()))
    ki = lax.broadcasted_iota(jnp.int32, (t, t), 0)
    qq = lax.broadcasted_iota(jnp.int32, (t, t), 1)
    visible = ki <= qq
    heads = [(h, h) for h in range(N_HEADS)]

    def qk(h, kj, diagonal):
        k = k_ref[h, pl.ds(pl.multiple_of(kj * t, t), t), :]
        s = lax.dot_general(k, q_ref[h], trans_b, preferred_element_type=F32)
        return jnp.where(visible, s, -jnp.inf) if diagonal else s

    def finish():
        for i, h in heads:
            a = acc_ref[i]
            ot_ref[h * HEAD_DIM:(h + 1) * HEAD_DIM, :] = a[0:HEAD_DIM] * (1.0 / a[HEAD_DIM:HEAD_DIM + 1])

    def bounded_blocks(kjs, diagonal):
        for n, kj in enumerate(kjs):
            for i, h in heads:
                p_ref[n * N_HEADS + i] = jnp.exp2(qk(h, kj, diagonal) + fq_ref[h]).astype(BF16)
        for i, h in heads:
            acc_ref[i] += sum(jnp.dot(v_ref[h, kj], p_ref[n * N_HEADS + i], preferred_element_type=F32)
                              for n, kj in enumerate(kjs))

    def attend_bounded():
        def pair_body(j, _):
            bounded_blocks([2 * j, 2 * j + 1], False)
            return 0

        lax.fori_loop(0, qi // 2, pair_body, 0)

        @pl.when(qi % 2 == 1)
        def _():
            bounded_blocks([qi - 1], False)

        bounded_blocks([qi], True)
        finish()

    def running_max_blocks(kj, diagonal):
        s_max = []
        for i, h in heads:
            s = qk(h, kj, diagonal)
            s_ref[i] = s
            s_max.append(jnp.max(s, axis=0, keepdims=True))
        for (i, h), sm in zip(heads, s_max):
            fq = fq_ref[h]
            m_old = m_ref[i]
            m_new = jnp.maximum(m_old, sm + fq)
            alpha = jnp.exp2(m_old - m_new)
            p = jnp.exp2(s_ref[i] + (fq - m_new)).astype(BF16)
            acc_ref[i] = alpha * acc_ref[i] + jnp.dot(v_ref[h, kj], p, preferred_element_type=F32)
            m_ref[i] = m_new

    def attend_running_max():
        def k_body(kj, _):
            running_max_blocks(kj, False)
            return 0

        lax.fori_loop(0, qi, k_body, 0)
        running_max_blocks(qi, True)
        finish()

    for i, _h in heads:
        acc_ref[i] = jnp.zeros((V_ROWS, t), F32)

    @pl.when(bounded_ref[0] == 1)
    def _():
        attend_bounded()

    @pl.when(bounded_ref[0] != 1)
    def _():
        for i, _h in heads:
            m_ref[i] = jnp.full((1, t), -jnp.inf, F32)
        attend_running_max()

    yt = ot_ref[...] * szt_ref[...].astype(F32)
    y = yt.T.astype(BF16)
    out = jnp.dot(y, w_ref[...], preferred_element_type=F32)
    gate = mb_ref[...][:, 2 * D_MODEL:]
    o_ref[...] = x_ref[...] + gate * out


def _fox_attn(bounded, qaug, kaug, vt, ft, szt, x1, mb, w_out):
    t = ATT_T
    return pl.pallas_call(
        _attn_kernel,
        grid=(BATCH, SEQ // t),
        in_specs=[
            pl.BlockSpec(memory_space=pltpu.SMEM),
            pl.BlockSpec((None, N_HEADS, t, LANES), lambda b, i: (b, 0, i, 0)),
            pl.BlockSpec((None, N_HEADS, SEQ, LANES), lambda b, i: (b, 0, 0, 0)),
            pl.BlockSpec((None, N_HEADS, N_KT, V_ROWS, t), lambda b, i: (b, 0, 0, 0, 0)),
            pl.BlockSpec((None, N_HEADS, 1, t), lambda b, i: (b, 0, 0, i)),
            pl.BlockSpec((None, D_MODEL, t), lambda b, i: (b, 0, i)),
            pl.BlockSpec((None, t, D_MODEL), lambda b, i: (b, i, 0)),
            pl.BlockSpec((None, 1, 3 * D_MODEL), lambda b, i: (b, 0, 0)),
            _const_spec((D_MODEL, D_MODEL)),
        ],
        out_specs=pl.BlockSpec((None, t, D_MODEL), lambda b, i: (b, i, 0)),
        out_shape=jax.ShapeDtypeStruct((BATCH, SEQ, D_MODEL), F32),
        scratch_shapes=[
            pltpu.VMEM((N_HEADS, V_ROWS, t), F32),
            pltpu.VMEM((N_HEADS, 1, t), F32),
            pltpu.VMEM((D_MODEL, t), F32),
            pltpu.VMEM((N_HEADS, t, t), F32),
            pltpu.VMEM((2 * N_HEADS, t, t), BF16),
        ],
        compiler_params=_cparams(("arbitrary", "arbitrary")),
        name="fox_attn",
    )(bounded, qaug, kaug, vt, ft, szt, x1, mb, w_out)


def kernel(x, c, a_norm_g, a_mod_w, a_mod_b, a_w_in, a_log_dt, a_A_re, a_A_im, a_B_re, a_B_im,
           a_C_re, a_C_im, a_D, a_w_glu, a_b_glu, a_w_out, kv_norm_g, kv_mod_w, kv_mod_b, kv_w,
           kv_f_bias, k_norm_g, b_norm_g, b_mod_w, b_mod_b, b_w_in, q_norm_g, b_w_out):
    assert x.shape == (BATCH, SEQ, D_MODEL) and a_mod_w.shape[0] == 1 and b_mod_w.shape[0] == 1
    aw = N_HEADS * HEAD_DIM

    mod_a, mod_kv, mod_b = _modulation(c, a_mod_w[0], a_mod_b[0], kv_mod_w, kv_mod_b,
                                       b_mod_w[0], b_mod_b[0])

    w2, wc, cb, a2r, a2i = _s5_params(a_log_dt[0], a_A_re[0], a_A_im[0], a_B_re[0], a_B_im[0],
                                      a_C_re[0], a_C_im[0])
    x1 = _s5_layer(x, mod_a, a_norm_g[0].reshape(1, D_MODEL), a_w_in[0].astype(BF16), w2, a2r, a2i,
                   wc, cb, a_D[0].reshape(1, D_MODEL), a_w_glu[0].astype(BF16),
                   a_b_glu[0].reshape(1, D_MODEL), a_w_out[0].astype(BF16))

    wk = kv_w[:, :aw].astype(BF16)
    wvt = kv_w[:, aw:2 * aw].T.astype(BF16)
    wf = jnp.pad(jnp.repeat(kv_w[:, 2 * aw:], 3, axis=1),
                 ((0, 0), (0, LANES - 3 * N_HEADS))).astype(BF16)
    fb = jnp.pad(jnp.repeat(kv_f_bias, 3), (0, LANES - 3 * N_HEADS)).reshape(1, LANES)
    part = (jnp.arange(LANES, dtype=jnp.int32) % 3).reshape(1, LANES)
    wq = b_w_in[0][:, :aw].astype(BF16)
    wzt = b_w_in[0][:, aw:].T.astype(BF16)
    kng = jnp.tile(k_norm_g, 2).reshape(1, LANES)
    qng = (jnp.tile(q_norm_g[0], 2) * (HEAD_DIM ** -0.5 * LOG2E)).reshape(1, LANES)
    mkv3 = mod_kv.reshape(BATCH, 1, 2 * D_MODEL)
    mb3 = mod_b.reshape(BATCH, 1, 3 * D_MODEL)
    kaug, qaug, vt, szt, ft = _fox_proj(x1, mkv3, mb3, kv_norm_g.reshape(1, D_MODEL),
                                        b_norm_g[0].reshape(1, D_MODEL), wk, wvt, wf, fb,
                                        wq, wzt, kng, qng, part)
    qk_bound = HEAD_DIM * jnp.max(jnp.abs(kng)) * jnp.max(jnp.abs(qng))
    bounded = (qk_bound <= MAX_UNSTABILISED_LOG2).astype(jnp.int32).reshape(1)
    return _fox_attn(bounded, qaug, kaug, vt, ft, szt, x1, mb3, b_w_out[0].astype(BF16))
```

```python
import jax
import jax.numpy as jnp
from jax import lax
from jax.experimental import pallas as pl
from jax.experimental.pallas import tpu as pltpu

D_MODEL = 1024
BATCH = 8
SEQ = 2048
GROUP = 16
N_GROUPS = D_MODEL // GROUP
STATE = 64
N_STATES = N_GROUPS * STATE
N_HEADS = 16
HEAD_DIM = 64
EPS = 1e-6

F32 = jnp.float32
BF16 = jnp.bfloat16

SUBLANES = 8
LANES = 128
MXU_DIM = 256
VMEM_LIMIT_BYTES = 56 * 1024 * 1024

S5_T = 64
S5_PAIRS = S5_T // 2
S5_CH = LANES
N_CH = D_MODEL // S5_CH
CH_GROUPS = S5_CH // GROUP
CH_STATES = CH_GROUPS * STATE
PROJ_TM = 512
ATT_T = 256
N_KT = SEQ // ATT_T
KT_PER_PROJ = PROJ_TM // ATT_T
MAX_UNSTABILISED_LOG2 = 64.0
V_ROWS = HEAD_DIM + 16
BIAS_LANE = HEAD_DIM
MOD_BN = 512
MOD_KV_BLOCKS = 2 * D_MODEL // MOD_BN
LOG2E = 1.4426950408889634


def _cparams(sem):
    return pltpu.CompilerParams(dimension_semantics=sem, vmem_limit_bytes=VMEM_LIMIT_BYTES)


def _const_spec(shape):
    nd = len(shape)
    return pl.BlockSpec(shape, lambda *_: (0,) * nd, pipeline_mode=pl.Buffered(1))


def _col_block_spec(j):
    return pl.BlockSpec((D_MODEL, D_MODEL), lambda *_: (0, j), pipeline_mode=pl.Buffered(1))


def _split3(x):
    hi = x.astype(BF16).astype(F32)
    r = x - hi
    mid = r.astype(BF16).astype(F32)
    lo = (r - mid).astype(BF16).astype(F32)
    return hi, mid, lo


def _mod_kernel(c_ref, wa_ref, ba_ref, wkv_ref, bkv_ref, wb_ref, bb_ref, oa_ref, okv_ref, ob_ref):
    c = c_ref[...]
    s = (c * jax.nn.sigmoid(c)).astype(BF16)

    def site(w_ref, b_ref, o_ref):
        o_ref[...] = jnp.dot(s, w_ref[...].astype(BF16), preferred_element_type=F32) + b_ref[...]

    site(wa_ref, ba_ref, oa_ref)
    site(wb_ref, bb_ref, ob_ref)

    @pl.when(pl.program_id(0) < MOD_KV_BLOCKS)
    def _():
        site(wkv_ref, bkv_ref, okv_ref)


def _modulation(c, wa, ba, wkv, bkv, wb, bb):
    col = lambda j: (0, j)
    col_kv = lambda j: (0, jnp.minimum(j, MOD_KV_BLOCKS - 1))
    n3, n2 = 3 * D_MODEL, 2 * D_MODEL
    return pl.pallas_call(
        _mod_kernel,
        grid=(n3 // MOD_BN,),
        in_specs=[
            pl.BlockSpec((BATCH, D_MODEL), lambda j: (0, 0)),
            pl.BlockSpec((D_MODEL, MOD_BN), col),
            pl.BlockSpec((1, MOD_BN), col),
            pl.BlockSpec((D_MODEL, MOD_BN), col_kv),
            pl.BlockSpec((1, MOD_BN), col_kv),
            pl.BlockSpec((D_MODEL, MOD_BN), col),
            pl.BlockSpec((1, MOD_BN), col),
        ],
        out_specs=[
            pl.BlockSpec((BATCH, MOD_BN), col),
            pl.BlockSpec((BATCH, MOD_BN), col_kv),
            pl.BlockSpec((BATCH, MOD_BN), col),
        ],
        out_shape=[
            jax.ShapeDtypeStruct((BATCH, n3), F32),
            jax.ShapeDtypeStruct((BATCH, n2), F32),
            jax.ShapeDtypeStruct((BATCH, n3), F32),
        ],
        compiler_params=_cparams(("arbitrary",)),
        name="modulation",
    )(c, wa, ba.reshape(1, n3), wkv, bkv.reshape(1, n2), wb, bb.reshape(1, n3))


def _s5_kernel(x_ref, mod_ref, g_ref, win_ref, w2_ref, a2r_ref, a2i_ref, wc_ref, cb_ref, d_ref,
               wglu_ref, bglu_ref, wout_ref, o_ref, sre, sim, st_re, st_im):
    tm = S5_T * BATCH
    mh = S5_PAIRS * BATCH

    @pl.when(pl.program_id(0) == 0)
    def _():
        st_re[...] = jnp.zeros_like(st_re)
        st_im[...] = jnp.zeros_like(st_im)

    x4 = jnp.swapaxes(x_ref[...], 0, 1).reshape(S5_PAIRS, 2, BATCH, D_MODEL)
    x3 = jnp.concatenate([x4[:, 0], x4[:, 1]], axis=0)
    mod = mod_ref[...]
    shift = mod[:, :D_MODEL]
    scale = mod[:, D_MODEL:2 * D_MODEL]
    gate = mod[:, 2 * D_MODEL:]
    ms = jnp.mean(x3 * x3, axis=-1, keepdims=True)
    h3 = (x3 * lax.rsqrt(ms + EPS)) * (g_ref[...] * (1.0 + scale))[None] + shift[None]
    h = h3.reshape(tm, D_MODEL).astype(BF16)
    uz = jnp.dot(h, win_ref[...], preferred_element_type=F32)
    u = uz[:, :D_MODEL]
    z = uz[:, D_MODEL:]
    ub = u.astype(BF16)
    ue = ub[:mh]
    uo = ub[mh:]

    for c in range(N_CH):
        cols = slice(c * S5_CH, (c + 1) * S5_CH)
        lhs = jnp.concatenate([ue[:, cols], uo[:, cols]], axis=1)
        p = jnp.dot(lhs, w2_ref[c], preferred_element_type=F32)
        sre[c, BATCH:, :] = p[:, :CH_STATES]
        sim[c, BATCH:, :] = p[:, CH_STATES:]

    for c in range(N_CH):
        a2r = a2r_ref[c]
        a2i = a2i_ref[c]
        sr = st_re[c]
        si = st_im[c]
        sre[c, 0:BATCH, :] = sr
        sim[c, 0:BATCH, :] = si
        for m in range(S5_PAIRS):
            rows = pl.ds((m + 1) * BATCH, BATCH)
            sr, si = (a2r * sr - a2i * si + sre[c, rows, :],
                      a2r * si + a2i * sr + sim[c, rows, :])
            sre[c, rows, :] = sr
            sim[c, rows, :] = si
        st_re[c] = sr
        st_im[c] = si

    y_even, y_odd = [], []
    for c in range(N_CH):
        res = (jnp.dot(sre[c].astype(BF16), wc_ref[c, :CH_STATES, :], preferred_element_type=F32)
               + jnp.dot(sim[c].astype(BF16), wc_ref[c, CH_STATES:, :], preferred_element_type=F32))
        direct = jnp.dot(ue[:, c * S5_CH:(c + 1) * S5_CH], cb_ref[c], preferred_element_type=F32)
        y_odd.append(res[BATCH:, :S5_CH])
        y_even.append(res[:mh, S5_CH:] + direct)
    y = jnp.concatenate([jnp.concatenate(y_even, axis=1), jnp.concatenate(y_odd, axis=1)], axis=0)
    y = y + d_ref[...] * u
    y = jax.nn.gelu(y)
    gl = jnp.dot(y.astype(BF16), wglu_ref[...], preferred_element_type=F32) + bglu_ref[...]
    y = y * jax.nn.sigmoid(gl)
    y = y * (z * jax.nn.sigmoid(z))
    o = jnp.dot(y.astype(BF16), wout_ref[...], preferred_element_type=F32)
    out3 = x3 + gate[None] * o.reshape(S5_T, BATCH, D_MODEL)
    out3 = jnp.stack([out3[:S5_PAIRS], out3[S5_PAIRS:]], axis=1).reshape(S5_T, BATCH, D_MODEL)
    o_ref[...] = jnp.swapaxes(out3, 0, 1)


def _s5_layer(x, mod, g, w_in, w2, a2r, a2i, wc, cb, dvec, w_glu, b_glu, w_out):
    buf_rows = (S5_PAIRS + 1) * BATCH
    return pl.pallas_call(
        _s5_kernel,
        grid=(SEQ // S5_T,),
        in_specs=[
            pl.BlockSpec((BATCH, S5_T, D_MODEL), lambda i: (0, i, 0)),
            _const_spec((BATCH, 3 * D_MODEL)),
            _const_spec((1, D_MODEL)),
            _const_spec((D_MODEL, 2 * D_MODEL)),
            _const_spec((N_CH, 2 * S5_CH, 2 * CH_STATES)),
            _const_spec((N_CH, BATCH, CH_STATES)),
            _const_spec((N_CH, BATCH, CH_STATES)),
            _const_spec((N_CH, 2 * CH_STATES, 2 * S5_CH)),
            _const_spec((N_CH, S5_CH, S5_CH)),
            _const_spec((1, D_MODEL)),
            _const_spec((D_MODEL, D_MODEL)),
            _const_spec((1, D_MODEL)),
            _const_spec((D_MODEL, D_MODEL)),
        ],
        out_specs=pl.BlockSpec((BATCH, S5_T, D_MODEL), lambda i: (0, i, 0)),
        out_shape=jax.ShapeDtypeStruct((BATCH, SEQ, D_MODEL), F32),
        scratch_shapes=[
            pltpu.VMEM((N_CH, buf_rows, CH_STATES), F32),
            pltpu.VMEM((N_CH, buf_rows, CH_STATES), F32),
            pltpu.VMEM((N_CH, BATCH, CH_STATES), F32),
            pltpu.VMEM((N_CH, BATCH, CH_STATES), F32),
        ],
        compiler_params=_cparams(("arbitrary",)),
        name="s5_layer",
    )(x, mod, g, w_in, w2, a2r, a2i, wc, cb, dvec, w_glu, b_glu, w_out)


def _s5_params(log_dt, a_re, a_im, b_re, b_im, c_re, c_im):
    dt = jnp.exp(log_dt)[:, None]
    mag = jnp.exp(a_re * dt)
    ar, ai = mag * jnp.cos(a_im * dt), mag * jnp.sin(a_im * dt)
    den = a_re * a_re + a_im * a_im
    nr = ar - 1.0
    coef_r = (nr * a_re + ai * a_im) / den
    coef_i = (ai * a_re - nr * a_im) / den
    bb_r = coef_r[..., None] * b_re - coef_i[..., None] * b_im
    bb_i = coef_r[..., None] * b_im + coef_i[..., None] * b_re
    a2r, a2i = ar * ar - ai * ai, 2.0 * ar * ai
    ab_r = ar[..., None] * bb_r - ai[..., None] * bb_i
    ab_i = ar[..., None] * bb_i + ai[..., None] * bb_r
    ca_r = c_re * ar[:, None, :] - c_im * ai[:, None, :]
    ca_i = c_re * ai[:, None, :] + c_im * ar[:, None, :]
    hi = lax.Precision.HIGHEST
    cb = (jnp.einsum('gcp,gpk->gck', c_re, bb_r, precision=hi)
          - jnp.einsum('gcp,gpk->gck', c_im, bb_i, precision=hi))

    def block_diag(t):
        r, q = t.shape[1], t.shape[2]
        t = t.reshape(N_CH, CH_GROUPS * r, q)
        rows_blk = jnp.arange(CH_GROUPS * r) // r
        cols_blk = jnp.arange(CH_GROUPS * q) // q
        return jnp.where(rows_blk[:, None] == cols_blk[None, :], jnp.tile(t, (1, 1, CH_GROUPS)), 0.0)

    def in_rows(b_r, b_i):
        return jnp.concatenate([block_diag(b_r.transpose(0, 2, 1)), block_diag(b_i.transpose(0, 2, 1))],
                               axis=2)

    def out_cols(c_r, c_i):
        return jnp.concatenate([block_diag(c_r.transpose(0, 2, 1)), block_diag(-c_i.transpose(0, 2, 1))],
                               axis=1)

    w2 = jnp.concatenate([in_rows(ab_r, ab_i), in_rows(bb_r, bb_i)], axis=1).astype(BF16)
    wc = jnp.concatenate([out_cols(c_re, c_im), out_cols(ca_r, ca_i)], axis=2).astype(BF16)
    cbp = block_diag(cb.transpose(0, 2, 1)).astype(BF16)

    def rows(a):
        return jnp.broadcast_to(a.reshape(N_CH, 1, CH_STATES), (N_CH, BATCH, CH_STATES))

    return w2, wc, cbp, rows(a2r), rows(a2i)


def _log_sigmoid(x):
    return jnp.minimum(x, 0.0) - jnp.log1p(jnp.exp(-jnp.abs(x)))


def _proj_kernel(x_ref, mkv_ref, mb_ref, gkv_ref, gb_ref, wk_ref, wvt_ref, wf_ref, fb_ref,
                 wq_ref, wz_ref, kng_ref, qng_ref, part_ref,
                 kaug_ref, qaug_ref, vt_ref, sz_ref, ft_ref, carry_ref):
    tm = PROJ_TM

    @pl.when(pl.program_id(1) == 0)
    def _():
        carry_ref[...] = jnp.zeros_like(carry_ref)

    x = x_ref[...]
    xn = x * lax.rsqrt(jnp.mean(x * x, axis=-1, keepdims=True) + EPS)
    mkv = mkv_ref[...]
    h2 = xn * (gkv_ref[...] * (1.0 + mkv[:, D_MODEL:])) + mkv[:, :D_MODEL]
    mb = mb_ref[...]
    h3 = xn * (gb_ref[...] * (1.0 + mb[:, D_MODEL:2 * D_MODEL])) + mb[:, :D_MODEL]
    h2b = h2.astype(BF16)
    h3b = h3.astype(BF16)
    trans_b = (((1,), (1,)), ((), ()))

    f = jnp.dot(h2b, wf_ref[...], preferred_element_type=F32) + fb_ref[...]
    ls = _log_sigmoid(f)
    ri = lax.broadcasted_iota(jnp.int32, (tm, tm), 0)
    ci = lax.broadcasted_iota(jnp.int32, (tm, tm), 1)
    tri = jnp.where(ci <= ri, 1.0, 0.0).astype(BF16)
    l_hi = ls.astype(BF16)
    l_lo = (ls - l_hi.astype(F32)).astype(BF16)
    fcum = (jnp.dot(tri, l_hi, preferred_element_type=F32)
            + jnp.dot(tri, l_lo, preferred_element_type=F32)) + carry_ref[...]
    carry_ref[...] = fcum[tm - 1:tm, :]

    f2 = fcum * LOG2E
    fct = f2.T
    for h in range(N_HEADS):
        ft_ref[h] = fct[3 * h:3 * h + 1, :]
    n_hi, n_mid, n_lo = _split3(-f2)
    part = part_ref[...]
    f_parts = jnp.where(part == 0, n_hi, jnp.where(part == 1, n_mid, n_lo))

    lane = lax.broadcasted_iota(jnp.int32, (tm, LANES), 1)
    low = lane < HEAD_DIM
    ones_cols = jnp.where(lane < BIAS_LANE + 3, 1.0, 0.0)
    kng = kng_ref[...]
    qng = qng_ref[...]

    def pair_scale(sq):
        ss_a = jnp.sum(jnp.where(low, sq, 0.0), axis=-1, keepdims=True)
        ss_b = jnp.sum(jnp.where(low, 0.0, sq), axis=-1, keepdims=True)
        return lax.rsqrt(jnp.where(low, ss_a, ss_b) * (1.0 / HEAD_DIM) + EPS)

    heads_per_chunk = MXU_DIM // HEAD_DIM
    pad_row = lax.broadcasted_iota(jnp.int32, (heads_per_chunk, V_ROWS - HEAD_DIM, ATT_T), 1)
    ones_rows = jnp.where(pad_row == 0, 1.0, 0.0).astype(BF16)
    for c in range(D_MODEL // MXU_DIM):
        cols = slice(c * MXU_DIM, (c + 1) * MXU_DIM)
        hs = slice(c * heads_per_chunk, (c + 1) * heads_per_chunk)
        k = jnp.dot(h2b, wk_ref[:, cols], preferred_element_type=F32)
        q = jnp.dot(h3b, wq_ref[:, cols], preferred_element_type=F32)
        for lp in range(MXU_DIM // LANES):
            kp = k[:, lp * LANES:(lp + 1) * LANES]
            qp = q[:, lp * LANES:(lp + 1) * LANES]
            knp = (kp * pair_scale(kp * kp)) * kng
            qnp = (qp * pair_scale(qp * qp)) * qng
            for half in range(2):
                h = c * heads_per_chunk + 2 * lp + half
                ka = knp if half == 0 else pltpu.roll(knp, HEAD_DIM, 1)
                qa = qnp if half == 0 else pltpu.roll(qnp, HEAD_DIM, 1)
                bias = pltpu.roll(f_parts, BIAS_LANE - 3 * h, 1)
                bias = jnp.where(lane < BIAS_LANE + 3, bias, 0.0)
                kaug_ref[h] = jnp.where(low, ka, bias).astype(BF16)
                qaug_ref[h] = jnp.where(low, qa, ones_cols).astype(BF16)

        vt = lax.dot_general(wvt_ref[cols, :], h2b, trans_b, preferred_element_type=F32)
        vt3 = vt.reshape(heads_per_chunk, HEAD_DIM, tm).astype(BF16)
        for kt in range(KT_PER_PROJ):
            vt_ref[hs, kt, 0:HEAD_DIM, :] = vt3[:, :, kt * ATT_T:(kt + 1) * ATT_T]
            vt_ref[hs, kt, HEAD_DIM:, :] = ones_rows
        z = jnp.dot(h3b, wz_ref[:, cols], preferred_element_type=F32)
        sz_ref[:, cols] = (z * jax.nn.sigmoid(z)).astype(BF16)


def _fox_proj(x1, mkv, mb, gkv, gb, wk, wvt, wf, fb, wq, wz, kng, qng, part):
    tm = PROJ_TM
    row = lambda b, t: (b, t, 0)
    per_b = lambda b, t: (b, 0, 0)
    return pl.pallas_call(
        _proj_kernel,
        grid=(BATCH, SEQ // tm),
        in_specs=[
            pl.BlockSpec((None, tm, D_MODEL), row),
            pl.BlockSpec((None, 1, 2 * D_MODEL), per_b),
            pl.BlockSpec((None, 1, 3 * D_MODEL), per_b),
            _const_spec((1, D_MODEL)),
            _const_spec((1, D_MODEL)),
            _col_block_spec(0),
            _const_spec((D_MODEL, D_MODEL)),
            _const_spec((D_MODEL, LANES)),
            _const_spec((1, LANES)),
            _col_block_spec(0),
            _col_block_spec(1),
            _const_spec((1, LANES)),
            _const_spec((1, LANES)),
            _const_spec((1, LANES)),
        ],
        out_specs=[
            pl.BlockSpec((None, N_HEADS, tm, LANES), lambda b, t: (b, 0, t, 0)),
            pl.BlockSpec((None, N_HEADS, tm, LANES), lambda b, t: (b, 0, t, 0)),
            pl.BlockSpec((None, N_HEADS, KT_PER_PROJ, V_ROWS, ATT_T), lambda b, t: (b, 0, t, 0, 0)),
            pl.BlockSpec((None, tm, D_MODEL), lambda b, t: (b, t, 0)),
            pl.BlockSpec((None, N_HEADS, 1, tm), lambda b, t: (b, 0, 0, t)),
        ],
        out_shape=[
            jax.ShapeDtypeStruct((BATCH, N_HEADS, SEQ, LANES), BF16),
            jax.ShapeDtypeStruct((BATCH, N_HEADS, SEQ, LANES), BF16),
            jax.ShapeDtypeStruct((BATCH, N_HEADS, N_KT, V_ROWS, ATT_T), BF16),
            jax.ShapeDtypeStruct((BATCH, SEQ, D_MODEL), BF16),
            jax.ShapeDtypeStruct((BATCH, N_HEADS, 1, SEQ), F32),
        ],
        scratch_shapes=[pltpu.VMEM((1, LANES), F32)],
        compiler_params=_cparams(("arbitrary", "arbitrary")),
        name="fox_proj",
    )(x1, mkv, mb, gkv, gb, wk, wvt, wf, fb, wq, wz, kng, qng, part)


def _attn_kernel(bounded_ref, q_ref, k_ref, v_ref, fq_ref, sz_ref, x_ref, mb_ref, w_ref, o_ref,
                 acc_ref, m_ref, ot_ref, s_ref, p_ref):
    t = ATT_T
    qi = pl.program_id(1)
    trans_b = (((1,), (1,)), ((), ()))
    ki = lax.broadcasted_iota(jnp.int32, (t, t), 0)
    qq = lax.broadcasted_iota(jnp.int32, (t, t), 1)
    visible = ki <= qq
    heads = [(h, h) for h in range(N_HEADS)]

    def qk(h, kj, diagonal):
        k = k_ref[h, pl.ds(pl.multiple_of(kj * t, t), t), :]
        s = lax.dot_general(k, q_ref[h], trans_b, preferred_element_type=F32)
        return jnp.where(visible, s, -jnp.inf) if diagonal else s

    def finish():
        for i, h in heads:
            a = acc_ref[i]
            ot_ref[h * HEAD_DIM:(h + 1) * HEAD_DIM, :] = a[0:HEAD_DIM] * (1.0 / a[HEAD_DIM:HEAD_DIM + 1])

    def bounded_blocks(kjs, diagonal):
        for n, kj in enumerate(kjs):
            for i, h in heads:
                p_ref[n * N_HEADS + i] = jnp.exp2(qk(h, kj, diagonal) + fq_ref[h]).astype(BF16)
        for i, h in heads:
            acc_ref[i] += sum(jnp.dot(v_ref[h, kj], p_ref[n * N_HEADS + i], preferred_element_type=F32)
                              for n, kj in enumerate(kjs))

    def attend_bounded():
        def pair_body(j, _):
            bounded_blocks([2 * j, 2 * j + 1], False)
            return 0

        lax.fori_loop(0, qi // 2, pair_body, 0)

        @pl.when(qi % 2 == 1)
        def _():
            bounded_blocks([qi - 1], False)

        bounded_blocks([qi], True)
        finish()

    def running_max_blocks(kj, diagonal):
        s_max = []
        for i, h in heads:
            s = qk(h, kj, diagonal)
            s_ref[i] = s
            s_max.append(jnp.max(s, axis=0, keepdims=True))
        for (i, h), sm in zip(heads, s_max):
            fq = fq_ref[h]
            m_old = m_ref[i]
            m_new = jnp.maximum(m_old, sm + fq)
            alpha = jnp.exp2(m_old - m_new)
            p = jnp.exp2(s_ref[i] + (fq - m_new)).astype(BF16)
            acc_ref[i] = alpha * acc_ref[i] + jnp.dot(v_ref[h, kj], p, preferred_element_type=F32)
            m_ref[i] = m_new

    def attend_running_max():
        def k_body(kj, _):
            running_max_blocks(kj, False)
            return 0

        lax.fori_loop(0, qi, k_body, 0)
        running_max_blocks(qi, True)
        finish()

    for i, _h in heads:
        acc_ref[i] = jnp.zeros((V_ROWS, t), F32)

    @pl.when(bounded_ref[0] == 1)
    def _():
        attend_bounded()

    @pl.when(bounded_ref[0] != 1)
    def _():
        for i, _h in heads:
            m_ref[i] = jnp.full((1, t), -jnp.inf, F32)
        attend_running_max()

    y = (ot_ref[...].T * sz_ref[...].astype(F32)).astype(BF16)
    out = jnp.dot(y, w_ref[...], preferred_element_type=F32)
    gate = mb_ref[...][:, 2 * D_MODEL:]
    o_ref[...] = x_ref[...] + gate * out


def _fox_attn(bounded, qaug, kaug, vt, ft, sz, x1, mb, w_out):
    t = ATT_T
    return pl.pallas_call(
        _attn_kernel,
        grid=(BATCH, SEQ // t),
        in_specs=[
            pl.BlockSpec(memory_space=pltpu.SMEM),
            pl.BlockSpec((None, N_HEADS, t, LANES), lambda b, i: (b, 0, i, 0)),
            pl.BlockSpec((None, N_HEADS, SEQ, LANES), lambda b, i: (b, 0, 0, 0)),
            pl.BlockSpec((None, N_HEADS, N_KT, V_ROWS, t), lambda b, i: (b, 0, 0, 0, 0)),
            pl.BlockSpec((None, N_HEADS, 1, t), lambda b, i: (b, 0, 0, i)),
            pl.BlockSpec((None, t, D_MODEL), lambda b, i: (b, i, 0)),
            pl.BlockSpec((None, t, D_MODEL), lambda b, i: (b, i, 0)),
            pl.BlockSpec((None, 1, 3 * D_MODEL), lambda b, i: (b, 0, 0)),
            _const_spec((D_MODEL, D_MODEL)),
        ],
        out_specs=pl.BlockSpec((None, t, D_MODEL), lambda b, i: (b, i, 0)),
        out_shape=jax.ShapeDtypeStruct((BATCH, SEQ, D_MODEL), F32),
        scratch_shapes=[
            pltpu.VMEM((N_HEADS, V_ROWS, t), F32),
            pltpu.VMEM((N_HEADS, 1, t), F32),
            pltpu.VMEM((D_MODEL, t), F32),
            pltpu.VMEM((N_HEADS, t, t), F32),
            pltpu.VMEM((2 * N_HEADS, t, t), BF16),
        ],
        compiler_params=_cparams(("arbitrary", "arbitrary")),
        name="fox_attn",
    )(bounded, qaug, kaug, vt, ft, sz, x1, mb, w_out)


def kernel(x, c, a_norm_g, a_mod_w, a_mod_b, a_w_in, a_log_dt, a_A_re, a_A_im, a_B_re, a_B_im,
           a_C_re, a_C_im, a_D, a_w_glu, a_b_glu, a_w_out, kv_norm_g, kv_mod_w, kv_mod_b, kv_w,
           kv_f_bias, k_norm_g, b_norm_g, b_mod_w, b_mod_b, b_w_in, q_norm_g, b_w_out):
    assert x.shape == (BATCH, SEQ, D_MODEL) and a_mod_w.shape[0] == 1 and b_mod_w.shape[0] == 1
    aw = N_HEADS * HEAD_DIM

    mod_a, mod_kv, mod_b = _modulation(c, a_mod_w[0], a_mod_b[0], kv_mod_w, kv_mod_b,
                                       b_mod_w[0], b_mod_b[0])

    w2, wc, cb, a2r, a2i = _s5_params(a_log_dt[0], a_A_re[0], a_A_im[0], a_B_re[0], a_B_im[0],
                                      a_C_re[0], a_C_im[0])
    x1 = _s5_layer(x, mod_a, a_norm_g[0].reshape(1, D_MODEL), a_w_in[0].astype(BF16), w2, a2r, a2i,
                   wc, cb, a_D[0].reshape(1, D_MODEL), a_w_glu[0].astype(BF16),
                   a_b_glu[0].reshape(1, D_MODEL), a_w_out[0].astype(BF16))

    kvb = kv_w.astype(BF16)
    wvt = kv_w[:, aw:2 * aw].T.astype(BF16)
    wf = jnp.pad(jnp.repeat(kv_w[:, 2 * aw:], 3, axis=1),
                 ((0, 0), (0, LANES - 3 * N_HEADS))).astype(BF16)
    fb = jnp.pad(jnp.repeat(kv_f_bias, 3), (0, LANES - 3 * N_HEADS)).reshape(1, LANES)
    part = (jnp.arange(LANES, dtype=jnp.int32) % 3).reshape(1, LANES)
    wqz = b_w_in[0].astype(BF16)
    kng = jnp.tile(k_norm_g, 2).reshape(1, LANES)
    qng = (jnp.tile(q_norm_g[0], 2) * (HEAD_DIM ** -0.5 * LOG2E)).reshape(1, LANES)
    mkv3 = mod_kv.reshape(BATCH, 1, 2 * D_MODEL)
    mb3 = mod_b.reshape(BATCH, 1, 3 * D_MODEL)
    kaug, qaug, vt, sz, ft = _fox_proj(x1, mkv3, mb3, kv_norm_g.reshape(1, D_MODEL),
                                       b_norm_g[0].reshape(1, D_MODEL), kvb, wvt, wf, fb,
                                       wqz, wqz, kng, qng, part)
    qk_bound = HEAD_DIM * jnp.max(jnp.abs(kng)) * jnp.max(jnp.abs(qng))
    bounded = (qk_bound <= MAX_UNSTABILISED_LOG2).astype(jnp.int32).reshape(1)
    return _fox_attn(bounded, qaug, kaug, vt, ft, sz, x1, mb3, b_w_out[0].astype(BF16))
```

```python
import jax
import jax.numpy as jnp
from jax import lax
from jax.experimental import pallas as pl
from jax.experimental.pallas import tpu as pltpu

D_MODEL = 1024
BATCH = 8
SEQ = 2048
GROUP = 16
N_GROUPS = D_MODEL // GROUP
STATE = 64
N_STATES = N_GROUPS * STATE
N_HEADS = 16
HEAD_DIM = 64
EPS = 1e-6

F32 = jnp.float32
BF16 = jnp.bfloat16

SUBLANES = 8
LANES = 128
MXU_DIM = 256
VMEM_LIMIT_BYTES = 58 * 1024 * 1024

S5_T = 64
S5_PAIRS = S5_T // 2
S5_CH = LANES
N_CH = D_MODEL // S5_CH
CH_GROUPS = S5_CH // GROUP
CH_STATES = CH_GROUPS * STATE
PROJ_TM = 512
ATT_T = 256
ATT_Q_SUB = 2
N_KT = SEQ // ATT_T
KT_PER_PROJ = PROJ_TM // ATT_T
MAX_UNSTABILISED_LOG2 = 64.0
V_ROWS = HEAD_DIM + 16
BIAS_LANE = HEAD_DIM
MOD_BN = 512
MOD_KV_BLOCKS = 2 * D_MODEL // MOD_BN
LOG2E = 1.4426950408889634


def _cparams(sem):
    return pltpu.CompilerParams(dimension_semantics=sem, vmem_limit_bytes=VMEM_LIMIT_BYTES)


def _const_spec(shape):
    nd = len(shape)
    return pl.BlockSpec(shape, lambda *_: (0,) * nd, pipeline_mode=pl.Buffered(1))


def _col_block_spec(j):
    return pl.BlockSpec((D_MODEL, D_MODEL), lambda *_: (0, j), pipeline_mode=pl.Buffered(1))


def _split3(x):
    hi = x.astype(BF16).astype(F32)
    r = x - hi
    mid = r.astype(BF16).astype(F32)
    lo = (r - mid).astype(BF16).astype(F32)
    return hi, mid, lo


def _mod_kernel(c_ref, wa_ref, ba_ref, wkv_ref, bkv_ref, wb_ref, bb_ref, oa_ref, okv_ref, ob_ref):
    c = c_ref[...]
    s = (c * jax.nn.sigmoid(c)).astype(BF16)

    def site(w_ref, b_ref, o_ref):
        o_ref[...] = jnp.dot(s, w_ref[...].astype(BF16), preferred_element_type=F32) + b_ref[...]

    site(wa_ref, ba_ref, oa_ref)
    site(wb_ref, bb_ref, ob_ref)

    @pl.when(pl.program_id(0) < MOD_KV_BLOCKS)
    def _():
        site(wkv_ref, bkv_ref, okv_ref)


def _modulation(c, wa, ba, wkv, bkv, wb, bb):
    col = lambda j: (0, j)
    col_kv = lambda j: (0, jnp.minimum(j, MOD_KV_BLOCKS - 1))
    n3, n2 = 3 * D_MODEL, 2 * D_MODEL
    return pl.pallas_call(
        _mod_kernel,
        grid=(n3 // MOD_BN,),
        in_specs=[
            pl.BlockSpec((BATCH, D_MODEL), lambda j: (0, 0)),
            pl.BlockSpec((D_MODEL, MOD_BN), col),
            pl.BlockSpec((1, MOD_BN), col),
            pl.BlockSpec((D_MODEL, MOD_BN), col_kv),
            pl.BlockSpec((1, MOD_BN), col_kv),
            pl.BlockSpec((D_MODEL, MOD_BN), col),
            pl.BlockSpec((1, MOD_BN), col),
        ],
        out_specs=[
            pl.BlockSpec((BATCH, MOD_BN), col),
            pl.BlockSpec((BATCH, MOD_BN), col_kv),
            pl.BlockSpec((BATCH, MOD_BN), col),
        ],
        out_shape=[
            jax.ShapeDtypeStruct((BATCH, n3), F32),
            jax.ShapeDtypeStruct((BATCH, n2), F32),
            jax.ShapeDtypeStruct((BATCH, n3), F32),
        ],
        compiler_params=_cparams(("arbitrary",)),
        name="modulation",
    )(c, wa, ba.reshape(1, n3), wkv, bkv.reshape(1, n2), wb, bb.reshape(1, n3))


def _s5_kernel(x_ref, mod_ref, g_ref, win_ref, w2_ref, a2r_ref, a2i_ref, wc_ref, cb_ref, d_ref,
               wglu_ref, bglu_ref, wout_ref, o_ref, sre, sim, st_re, st_im):
    tm = S5_T * BATCH
    mh = S5_PAIRS * BATCH

    @pl.when(pl.program_id(0) == 0)
    def _():
        st_re[...] = jnp.zeros_like(st_re)
        st_im[...] = jnp.zeros_like(st_im)

    x4 = jnp.swapaxes(x_ref[...], 0, 1).reshape(S5_PAIRS, 2, BATCH, D_MODEL)
    x3 = jnp.concatenate([x4[:, 0], x4[:, 1]], axis=0)
    mod = mod_ref[...]
    shift = mod[:, :D_MODEL]
    scale = mod[:, D_MODEL:2 * D_MODEL]
    gate = mod[:, 2 * D_MODEL:]
    ms = jnp.mean(x3 * x3, axis=-1, keepdims=True)
    h3 = (x3 * lax.rsqrt(ms + EPS)) * (g_ref[...] * (1.0 + scale))[None] + shift[None]
    h = h3.reshape(tm, D_MODEL).astype(BF16)
    uz = jnp.dot(h, win_ref[...], preferred_element_type=F32)
    u = uz[:, :D_MODEL]
    z = uz[:, D_MODEL:]
    ub = u.astype(BF16)
    ue = ub[:mh]
    uo = ub[mh:]

    for c in range(N_CH):
        cols = slice(c * S5_CH, (c + 1) * S5_CH)
        lhs = jnp.concatenate([ue[:, cols], uo[:, cols]], axis=1)
        p = jnp.dot(lhs, w2_ref[c], preferred_element_type=F32)
        sre[c, BATCH:, :] = p[:, :CH_STATES]
        sim[c, BATCH:, :] = p[:, CH_STATES:]

    for c in range(N_CH):
        a2r = a2r_ref[c]
        a2i = a2i_ref[c]
        sr = st_re[c]
        si = st_im[c]
        sre[c, 0:BATCH, :] = sr
        sim[c, 0:BATCH, :] = si
        for m in range(S5_PAIRS):
            rows = pl.ds((m + 1) * BATCH, BATCH)
            sr, si = (a2r * sr - a2i * si + sre[c, rows, :],
                      a2r * si + a2i * sr + sim[c, rows, :])
            sre[c, rows, :] = sr
            sim[c, rows, :] = si
        st_re[c] = sr
        st_im[c] = si

    y_even, y_odd = [], []
    for c in range(N_CH):
        res = (jnp.dot(sre[c].astype(BF16), wc_ref[c, :CH_STATES, :], preferred_element_type=F32)
               + jnp.dot(sim[c].astype(BF16), wc_ref[c, CH_STATES:, :], preferred_element_type=F32))
        direct = jnp.dot(ue[:, c * S5_CH:(c + 1) * S5_CH], cb_ref[c], preferred_element_type=F32)
        y_odd.append(res[BATCH:, :S5_CH])
        y_even.append(res[:mh, S5_CH:] + direct)
    y = jnp.concatenate([jnp.concatenate(y_even, axis=1), jnp.concatenate(y_odd, axis=1)], axis=0)
    y = y + d_ref[...] * u
    y = jax.nn.gelu(y)
    gl = jnp.dot(y.astype(BF16), wglu_ref[...], preferred_element_type=F32) + bglu_ref[...]
    y = y * jax.nn.sigmoid(gl)
    y = y * (z * jax.nn.sigmoid(z))
    o = jnp.dot(y.astype(BF16), wout_ref[...], preferred_element_type=F32)
    out3 = x3 + gate[None] * o.reshape(S5_T, BATCH, D_MODEL)
    out3 = jnp.stack([out3[:S5_PAIRS], out3[S5_PAIRS:]], axis=1).reshape(S5_T, BATCH, D_MODEL)
    o_ref[...] = jnp.swapaxes(out3, 0, 1)


def _s5_layer(x, mod, g, w_in, w2, a2r, a2i, wc, cb, dvec, w_glu, b_glu, w_out):
    buf_rows = (S5_PAIRS + 1) * BATCH
    return pl.pallas_call(
        _s5_kernel,
        grid=(SEQ // S5_T,),
        in_specs=[
            pl.BlockSpec((BATCH, S5_T, D_MODEL), lambda i: (0, i, 0)),
            _const_spec((BATCH, 3 * D_MODEL)),
            _const_spec((1, D_MODEL)),
            _const_spec((D_MODEL, 2 * D_MODEL)),
            _const_spec((N_CH, 2 * S5_CH, 2 * CH_STATES)),
            _const_spec((N_CH, BATCH, CH_STATES)),
            _const_spec((N_CH, BATCH, CH_STATES)),
            _const_spec((N_CH, 2 * CH_STATES, 2 * S5_CH)),
            _const_spec((N_CH, S5_CH, S5_CH)),
            _const_spec((1, D_MODEL)),
            _const_spec((D_MODEL, D_MODEL)),
            _const_spec((1, D_MODEL)),
            _const_spec((D_MODEL, D_MODEL)),
        ],
        out_specs=pl.BlockSpec((BATCH, S5_T, D_MODEL), lambda i: (0, i, 0)),
        out_shape=jax.ShapeDtypeStruct((BATCH, SEQ, D_MODEL), F32),
        scratch_shapes=[
            pltpu.VMEM((N_CH, buf_rows, CH_STATES), F32),
            pltpu.VMEM((N_CH, buf_rows, CH_STATES), F32),
            pltpu.VMEM((N_CH, BATCH, CH_STATES), F32),
            pltpu.VMEM((N_CH, BATCH, CH_STATES), F32),
        ],
        compiler_params=_cparams(("arbitrary",)),
        name="s5_layer",
    )(x, mod, g, w_in, w2, a2r, a2i, wc, cb, dvec, w_glu, b_glu, w_out)


def _s5_params(log_dt, a_re, a_im, b_re, b_im, c_re, c_im):
    dt = jnp.exp(log_dt)[:, None]
    mag = jnp.exp(a_re * dt)
    ar, ai = mag * jnp.cos(a_im * dt), mag * jnp.sin(a_im * dt)
    den = a_re * a_re + a_im * a_im
    nr = ar - 1.0
    coef_r = (nr * a_re + ai * a_im) / den
    coef_i = (ai * a_re - nr * a_im) / den
    bb_r = coef_r[..., None] * b_re - coef_i[..., None] * b_im
    bb_i = coef_r[..., None] * b_im + coef_i[..., None] * b_re
    a2r, a2i = ar * ar - ai * ai, 2.0 * ar * ai
    ab_r = ar[..., None] * bb_r - ai[..., None] * bb_i
    ab_i = ar[..., None] * bb_i + ai[..., None] * bb_r
    ca_r = c_re * ar[:, None, :] - c_im * ai[:, None, :]
    ca_i = c_re * ai[:, None, :] + c_im * ar[:, None, :]
    hi = lax.Precision.HIGHEST
    cb = (jnp.einsum('gcp,gpk->gck', c_re, bb_r, precision=hi)
          - jnp.einsum('gcp,gpk->gck', c_im, bb_i, precision=hi))

    def block_diag(t):
        r, q = t.shape[1], t.shape[2]
        t = t.reshape(N_CH, CH_GROUPS * r, q)
        rows_blk = jnp.arange(CH_GROUPS * r) // r
        cols_blk = jnp.arange(CH_GROUPS * q) // q
        return jnp.where(rows_blk[:, None] == cols_blk[None, :], jnp.tile(t, (1, 1, CH_GROUPS)), 0.0)

    def in_rows(b_r, b_i):
        return jnp.concatenate([block_diag(b_r.transpose(0, 2, 1)), block_diag(b_i.transpose(0, 2, 1))],
                               axis=2)

    def out_cols(c_r, c_i):
        return jnp.concatenate([block_diag(c_r.transpose(0, 2, 1)), block_diag(-c_i.transpose(0, 2, 1))],
                               axis=1)

    w2 = jnp.concatenate([in_rows(ab_r, ab_i), in_rows(bb_r, bb_i)], axis=1).astype(BF16)
    wc = jnp.concatenate([out_cols(c_re, c_im), out_cols(ca_r, ca_i)], axis=2).astype(BF16)
    cbp = block_diag(cb.transpose(0, 2, 1)).astype(BF16)

    def rows(a):
        return jnp.broadcast_to(a.reshape(N_CH, 1, CH_STATES), (N_CH, BATCH, CH_STATES))

    return w2, wc, cbp, rows(a2r), rows(a2i)


def _log_sigmoid(x):
    return jnp.minimum(x, 0.0) - jnp.log1p(jnp.exp(-jnp.abs(x)))


def _proj_kernel(x_ref, mkv_ref, mb_ref, gkv_ref, gb_ref, wk_ref, wvt_ref, wf_ref, fb_ref,
                 wq_ref, wz_ref, kng_ref, qng_ref, part_ref,
                 kaug_ref, qaug_ref, vt_ref, sz_ref, ft_ref, carry_ref):
    tm = PROJ_TM

    @pl.when(pl.program_id(1) == 0)
    def _():
        carry_ref[...] = jnp.zeros_like(carry_ref)

    x = x_ref[...]
    xn = x * lax.rsqrt(jnp.mean(x * x, axis=-1, keepdims=True) + EPS)
    mkv = mkv_ref[...]
    h2 = xn * (gkv_ref[...] * (1.0 + mkv[:, D_MODEL:])) + mkv[:, :D_MODEL]
    mb = mb_ref[...]
    h3 = xn * (gb_ref[...] * (1.0 + mb[:, D_MODEL:2 * D_MODEL])) + mb[:, :D_MODEL]
    h2b = h2.astype(BF16)
    h3b = h3.astype(BF16)
    trans_b = (((1,), (1,)), ((), ()))

    f = jnp.dot(h2b, wf_ref[...], preferred_element_type=F32) + fb_ref[...]
    ls = _log_sigmoid(f)
    ri = lax.broadcasted_iota(jnp.int32, (tm, tm), 0)
    ci = lax.broadcasted_iota(jnp.int32, (tm, tm), 1)
    tri = jnp.where(ci <= ri, 1.0, 0.0).astype(BF16)
    l_hi = ls.astype(BF16)
    l_lo = (ls - l_hi.astype(F32)).astype(BF16)
    fcum = (jnp.dot(tri, l_hi, preferred_element_type=F32)
            + jnp.dot(tri, l_lo, preferred_element_type=F32)) + carry_ref[...]
    carry_ref[...] = fcum[tm - 1:tm, :]

    f2 = fcum * LOG2E
    fct = f2.T
    for h in range(N_HEADS):
        ft_ref[h] = fct[3 * h:3 * h + 1, :]
    n_hi, n_mid, n_lo = _split3(-f2)
    part = part_ref[...]
    f_parts = jnp.where(part == 0, n_hi, jnp.where(part == 1, n_mid, n_lo))

    lane = lax.broadcasted_iota(jnp.int32, (tm, LANES), 1)
    low = lane < HEAD_DIM
    ones_cols = jnp.where(lane < BIAS_LANE + 3, 1.0, 0.0)
    kng = kng_ref[...]
    qng = qng_ref[...]

    def pair_scale(sq):
        ss_a = jnp.sum(jnp.where(low, sq, 0.0), axis=-1, keepdims=True)
        ss_b = jnp.sum(jnp.where(low, 0.0, sq), axis=-1, keepdims=True)
        return lax.rsqrt(jnp.where(low, ss_a, ss_b) * (1.0 / HEAD_DIM) + EPS)

    heads_per_chunk = MXU_DIM // HEAD_DIM
    pad_row = lax.broadcasted_iota(jnp.int32, (heads_per_chunk, V_ROWS - HEAD_DIM, ATT_T), 1)
    ones_rows = jnp.where(pad_row == 0, 1.0, 0.0).astype(BF16)
    for c in range(D_MODEL // MXU_DIM):
        cols = slice(c * MXU_DIM, (c + 1) * MXU_DIM)
        hs = slice(c * heads_per_chunk, (c + 1) * heads_per_chunk)
        k = jnp.dot(h2b, wk_ref[:, cols], preferred_element_type=F32)
        q = jnp.dot(h3b, wq_ref[:, cols], preferred_element_type=F32)
        for lp in range(MXU_DIM // LANES):
            kp = k[:, lp * LANES:(lp + 1) * LANES]
            qp = q[:, lp * LANES:(lp + 1) * LANES]
            knp = (kp * pair_scale(kp * kp)) * kng
            qnp = (qp * pair_scale(qp * qp)) * qng
            for half in range(2):
                h = c * heads_per_chunk + 2 * lp + half
                ka = knp if half == 0 else pltpu.roll(knp, HEAD_DIM, 1)
                qa = qnp if half == 0 else pltpu.roll(qnp, HEAD_DIM, 1)
                bias = pltpu.roll(f_parts, BIAS_LANE - 3 * h, 1)
                bias = jnp.where(lane < BIAS_LANE + 3, bias, 0.0)
                kaug_ref[h] = jnp.where(low, ka, bias).astype(BF16)
                qaug_ref[h] = jnp.where(low, qa, ones_cols).astype(BF16)

        vt = lax.dot_general(wvt_ref[cols, :], h2b, trans_b, preferred_element_type=F32)
        vt3 = vt.reshape(heads_per_chunk, HEAD_DIM, tm).astype(BF16)
        for kt in range(KT_PER_PROJ):
            vt_ref[hs, kt, 0:HEAD_DIM, :] = vt3[:, :, kt * ATT_T:(kt + 1) * ATT_T]
            vt_ref[hs, kt, HEAD_DIM:, :] = ones_rows
        z = jnp.dot(h3b, wz_ref[:, cols], preferred_element_type=F32)
        sz_ref[:, cols] = (z * jax.nn.sigmoid(z)).astype(BF16)


def _fox_proj(x1, mkv, mb, gkv, gb, wk, wvt, wf, fb, wq, wz, kng, qng, part):
    tm = PROJ_TM
    row = lambda b, t: (b, t, 0)
    per_b = lambda b, t: (b, 0, 0)
    return pl.pallas_call(
        _proj_kernel,
        grid=(BATCH, SEQ // tm),
        in_specs=[
            pl.BlockSpec((None, tm, D_MODEL), row),
            pl.BlockSpec((None, 1, 2 * D_MODEL), per_b),
            pl.BlockSpec((None, 1, 3 * D_MODEL), per_b),
            _const_spec((1, D_MODEL)),
            _const_spec((1, D_MODEL)),
            _col_block_spec(0),
            _const_spec((D_MODEL, D_MODEL)),
            _const_spec((D_MODEL, LANES)),
            _const_spec((1, LANES)),
            _col_block_spec(0),
            _col_block_spec(1),
            _const_spec((1, LANES)),
            _const_spec((1, LANES)),
            _const_spec((1, LANES)),
        ],
        out_specs=[
            pl.BlockSpec((None, N_HEADS, tm, LANES), lambda b, t: (b, 0, t, 0)),
            pl.BlockSpec((None, N_HEADS, tm, LANES), lambda b, t: (b, 0, t, 0)),
            pl.BlockSpec((None, N_HEADS, KT_PER_PROJ, V_ROWS, ATT_T), lambda b, t: (b, 0, t, 0, 0)),
            pl.BlockSpec((None, tm, D_MODEL), lambda b, t: (b, t, 0)),
            pl.BlockSpec((None, N_HEADS, 1, tm), lambda b, t: (b, 0, 0, t)),
        ],
        out_shape=[
            jax.ShapeDtypeStruct((BATCH, N_HEADS, SEQ, LANES), BF16),
            jax.ShapeDtypeStruct((BATCH, N_HEADS, SEQ, LANES), BF16),
            jax.ShapeDtypeStruct((BATCH, N_HEADS, N_KT, V_ROWS, ATT_T), BF16),
            jax.ShapeDtypeStruct((BATCH, SEQ, D_MODEL), BF16),
            jax.ShapeDtypeStruct((BATCH, N_HEADS, 1, SEQ), F32),
        ],
        scratch_shapes=[pltpu.VMEM((1, LANES), F32)],
        compiler_params=_cparams(("arbitrary", "arbitrary")),
        name="fox_proj",
    )(x1, mkv, mb, gkv, gb, wk, wvt, wf, fb, wq, wz, kng, qng, part)


def _attn_kernel(bounded_ref, q_ref, k_ref, v_ref, fq_ref, sz_ref, x_ref, mb_ref, w_ref, o_ref,
                 acc_ref, m_ref, ot_ref, s_ref, p_ref):
    t = ATT_T
    trans_b = (((1,), (1,)), ((), ()))
    ki = lax.broadcasted_iota(jnp.int32, (t, t), 0)
    qq = lax.broadcasted_iota(jnp.int32, (t, t), 1)
    visible = ki <= qq
    heads = [(h, h) for h in range(N_HEADS)]

    def query_tile(sub):
        qi = pl.program_id(1) * ATT_Q_SUB + sub
        cols = slice(sub * t, (sub + 1) * t)

        def qk(h, kj, diagonal):
            k = k_ref[h, pl.ds(pl.multiple_of(kj * t, t), t), :]
            s = lax.dot_general(k, q_ref[h, cols, :], trans_b, preferred_element_type=F32)
            return jnp.where(visible, s, -jnp.inf) if diagonal else s

        def finish():
            for i, h in heads:
                a = acc_ref[i]
                ot_ref[h * HEAD_DIM:(h + 1) * HEAD_DIM, cols] = (
                    a[0:HEAD_DIM] * (1.0 / a[HEAD_DIM:HEAD_DIM + 1]))

        def bounded_blocks(kjs, diagonal):
            for n, kj in enumerate(kjs):
                for i, h in heads:
                    p_ref[n * N_HEADS + i] = jnp.exp2(qk(h, kj, diagonal) + fq_ref[h, :, cols]).astype(BF16)
            for i, h in heads:
                acc_ref[i] += sum(jnp.dot(v_ref[h, kj], p_ref[n * N_HEADS + i], preferred_element_type=F32)
                                  for n, kj in enumerate(kjs))

        def attend_bounded():
            def pair_body(j, _):
                bounded_blocks([2 * j, 2 * j + 1], False)
                return 0

            lax.fori_loop(0, qi // 2, pair_body, 0)

            @pl.when(qi % 2 == 1)
            def _():
                bounded_blocks([qi - 1], False)

            bounded_blocks([qi], True)
            finish()

        def running_max_blocks(kj, diagonal):
            s_max = []
            for i, h in heads:
                s = qk(h, kj, diagonal)
                s_ref[i] = s
                s_max.append(jnp.max(s, axis=0, keepdims=True))
            for (i, h), sm in zip(heads, s_max):
                fq = fq_ref[h, :, cols]
                m_old = m_ref[i]
                m_new = jnp.maximum(m_old, sm + fq)
                alpha = jnp.exp2(m_old - m_new)
                p = jnp.exp2(s_ref[i] + (fq - m_new)).astype(BF16)
                acc_ref[i] = alpha * acc_ref[i] + jnp.dot(v_ref[h, kj], p, preferred_element_type=F32)
                m_ref[i] = m_new

        def attend_running_max():
            def k_body(kj, _):
                running_max_blocks(kj, False)
                return 0

            lax.fori_loop(0, qi, k_body, 0)
            running_max_blocks(qi, True)
            finish()

        for i, _h in heads:
            acc_ref[i] = jnp.zeros((V_ROWS, t), F32)

        @pl.when(bounded_ref[0] == 1)
        def _():
            attend_bounded()

        @pl.when(bounded_ref[0] != 1)
        def _():
            for i, _h in heads:
                m_ref[i] = jnp.full((1, t), -jnp.inf, F32)
            attend_running_max()

    for sub in range(ATT_Q_SUB):
        query_tile(sub)

    y = (ot_ref[...].T * sz_ref[...].astype(F32)).astype(BF16)
    out = jnp.dot(y, w_ref[...], preferred_element_type=F32)
    gate = mb_ref[...][:, 2 * D_MODEL:]
    o_ref[...] = x_ref[...] + gate * out


def _fox_attn(bounded, qaug, kaug, vt, ft, sz, x1, mb, w_out):
    t = ATT_T
    tq = ATT_Q_SUB * t
    return pl.pallas_call(
        _attn_kernel,
        grid=(BATCH, SEQ // tq),
        in_specs=[
            pl.BlockSpec(memory_space=pltpu.SMEM),
            pl.BlockSpec((None, N_HEADS, tq, LANES), lambda b, i: (b, 0, i, 0)),
            pl.BlockSpec((None, N_HEADS, SEQ, LANES), lambda b, i: (b, 0, 0, 0)),
            pl.BlockSpec((None, N_HEADS, N_KT, V_ROWS, t), lambda b, i: (b, 0, 0, 0, 0)),
            pl.BlockSpec((None, N_HEADS, 1, tq), lambda b, i: (b, 0, 0, i)),
            pl.BlockSpec((None, tq, D_MODEL), lambda b, i: (b, i, 0)),
            pl.BlockSpec((None, tq, D_MODEL), lambda b, i: (b, i, 0)),
            pl.BlockSpec((None, 1, 3 * D_MODEL), lambda b, i: (b, 0, 0)),
            _const_spec((D_MODEL, D_MODEL)),
        ],
        out_specs=pl.BlockSpec((None, tq, D_MODEL), lambda b, i: (b, i, 0)),
        out_shape=jax.ShapeDtypeStruct((BATCH, SEQ, D_MODEL), F32),
        scratch_shapes=[
            pltpu.VMEM((N_HEADS, V_ROWS, t), F32),
            pltpu.VMEM((N_HEADS, 1, t), F32),
            pltpu.VMEM((D_MODEL, tq), F32),
            pltpu.VMEM((N_HEADS, t, t), F32),
            pltpu.VMEM((2 * N_HEADS, t, t), BF16),
        ],
        compiler_params=_cparams(("arbitrary", "arbitrary")),
        name="fox_attn",
    )(bounded, qaug, kaug, vt, ft, sz, x1, mb, w_out)


def kernel(x, c, a_norm_g, a_mod_w, a_mod_b, a_w_in, a_log_dt, a_A_re, a_A_im, a_B_re, a_B_im,
           a_C_re, a_C_im, a_D, a_w_glu, a_b_glu, a_w_out, kv_norm_g, kv_mod_w, kv_mod_b, kv_w,
           kv_f_bias, k_norm_g, b_norm_g, b_mod_w, b_mod_b, b_w_in, q_norm_g, b_w_out):
    assert x.shape == (BATCH, SEQ, D_MODEL) and a_mod_w.shape[0] == 1 and b_mod_w.shape[0] == 1
    aw = N_HEADS * HEAD_DIM

    mod_a, mod_kv, mod_b = _modulation(c, a_mod_w[0], a_mod_b[0], kv_mod_w, kv_mod_b,
                                       b_mod_w[0], b_mod_b[0])

    w2, wc, cb, a2r, a2i = _s5_params(a_log_dt[0], a_A_re[0], a_A_im[0], a_B_re[0], a_B_im[0],
                                      a_C_re[0], a_C_im[0])
    x1 = _s5_layer(x, mod_a, a_norm_g[0].reshape(1, D_MODEL), a_w_in[0].astype(BF16), w2, a2r, a2i,
                   wc, cb, a_D[0].reshape(1, D_MODEL), a_w_glu[0].astype(BF16),
                   a_b_glu[0].reshape(1, D_MODEL), a_w_out[0].astype(BF16))

    kvb = kv_w.astype(BF16)
    wvt = kv_w[:, aw:2 * aw].T.astype(BF16)
    wf = jnp.pad(jnp.repeat(kv_w[:, 2 * aw:], 3, axis=1),
                 ((0, 0), (0, LANES - 3 * N_HEADS))).astype(BF16)
    fb = jnp.pad(jnp.repeat(kv_f_bias, 3), (0, LANES - 3 * N_HEADS)).reshape(1, LANES)
    part = (jnp.arange(LANES, dtype=jnp.int32) % 3).reshape(1, LANES)
    wqz = b_w_in[0].astype(BF16)
    kng = jnp.tile(k_norm_g, 2).reshape(1, LANES)
    qng = (jnp.tile(q_norm_g[0], 2) * (HEAD_DIM ** -0.5 * LOG2E)).reshape(1, LANES)
    mkv3 = mod_kv.reshape(BATCH, 1, 2 * D_MODEL)
    mb3 = mod_b.reshape(BATCH, 1, 3 * D_MODEL)
    kaug, qaug, vt, sz, ft = _fox_proj(x1, mkv3, mb3, kv_norm_g.reshape(1, D_MODEL),
                                       b_norm_g[0].reshape(1, D_MODEL), kvb, wvt, wf, fb,
                                       wqz, wqz, kng, qng, part)
    qk_bound = HEAD_DIM * jnp.max(jnp.abs(kng)) * jnp.max(jnp.abs(qng))
    bounded = (qk_bound <= MAX_UNSTABILISED_LOG2).astype(jnp.int32).reshape(1)
    return _fox_attn(bounded, qaug, kaug, vt, ft, sz, x1, mb3, b_w_out[0].astype(BF16))
```

```python
import jax
import jax.numpy as jnp
from jax import lax
from jax.experimental import pallas as pl
from jax.experimental.pallas import tpu as pltpu

D_MODEL = 1024
BATCH = 8
SEQ = 2048
GROUP = 16
N_GROUPS = D_MODEL // GROUP
STATE = 64
N_STATES = N_GROUPS * STATE
N_HEADS = 16
HEAD_DIM = 64
EPS = 1e-6

F32 = jnp.float32
BF16 = jnp.bfloat16

SUBLANES = 8
LANES = 128
MXU_DIM = 256
VMEM_LIMIT_BYTES = 56 * 1024 * 1024
ATTN_VMEM_LIMIT_BYTES = 58 * 1024 * 1024

S5_T = 64
S5_PAIRS = S5_T // 2
S5_CH = LANES
N_CH = D_MODEL // S5_CH
CH_GROUPS = S5_CH // GROUP
CH_STATES = CH_GROUPS * STATE
PROJ_TM = 512
ATT_T = 256
ATT_Q_SUB = 2
N_KT = SEQ // ATT_T
KT_PER_PROJ = PROJ_TM // ATT_T
MAX_UNSTABILISED_LOG2 = 64.0
V_ROWS = HEAD_DIM + 16
BIAS_LANE = HEAD_DIM
MOD_BN = 512
MOD_KV_BLOCKS = 2 * D_MODEL // MOD_BN
LOG2E = 1.4426950408889634


def _cparams(sem, vmem_limit_bytes=None):
    return pltpu.CompilerParams(dimension_semantics=sem,
                                vmem_limit_bytes=vmem_limit_bytes or VMEM_LIMIT_BYTES)


def _const_spec(shape):
    nd = len(shape)
    return pl.BlockSpec(shape, lambda *_: (0,) * nd, pipeline_mode=pl.Buffered(1))


def _col_block_spec(j):
    return pl.BlockSpec((D_MODEL, D_MODEL), lambda *_: (0, j), pipeline_mode=pl.Buffered(1))


def _split3(x):
    hi = x.astype(BF16).astype(F32)
    r = x - hi
    mid = r.astype(BF16).astype(F32)
    lo = (r - mid).astype(BF16).astype(F32)
    return hi, mid, lo


def _mod_kernel(c_ref, wa_ref, ba_ref, wkv_ref, bkv_ref, wb_ref, bb_ref, oa_ref, okv_ref, ob_ref):
    c = c_ref[...]
    s = (c * jax.nn.sigmoid(c)).astype(BF16)

    def site(w_ref, b_ref, o_ref):
        o_ref[...] = jnp.dot(s, w_ref[...].astype(BF16), preferred_element_type=F32) + b_ref[...]

    site(wa_ref, ba_ref, oa_ref)
    site(wb_ref, bb_ref, ob_ref)

    @pl.when(pl.program_id(0) < MOD_KV_BLOCKS)
    def _():
        site(wkv_ref, bkv_ref, okv_ref)


def _modulation(c, wa, ba, wkv, bkv, wb, bb):
    col = lambda j: (0, j)
    col_kv = lambda j: (0, jnp.minimum(j, MOD_KV_BLOCKS - 1))
    n3, n2 = 3 * D_MODEL, 2 * D_MODEL
    return pl.pallas_call(
        _mod_kernel,
        grid=(n3 // MOD_BN,),
        in_specs=[
            pl.BlockSpec((BATCH, D_MODEL), lambda j: (0, 0)),
            pl.BlockSpec((D_MODEL, MOD_BN), col),
            pl.BlockSpec((1, MOD_BN), col),
            pl.BlockSpec((D_MODEL, MOD_BN), col_kv),
            pl.BlockSpec((1, MOD_BN), col_kv),
            pl.BlockSpec((D_MODEL, MOD_BN), col),
            pl.BlockSpec((1, MOD_BN), col),
        ],
        out_specs=[
            pl.BlockSpec((BATCH, MOD_BN), col),
            pl.BlockSpec((BATCH, MOD_BN), col_kv),
            pl.BlockSpec((BATCH, MOD_BN), col),
        ],
        out_shape=[
            jax.ShapeDtypeStruct((BATCH, n3), F32),
            jax.ShapeDtypeStruct((BATCH, n2), F32),
            jax.ShapeDtypeStruct((BATCH, n3), F32),
        ],
        compiler_params=_cparams(("arbitrary",)),
        name="modulation",
    )(c, wa, ba.reshape(1, n3), wkv, bkv.reshape(1, n2), wb, bb.reshape(1, n3))


def _s5_kernel(x_ref, mod_ref, g_ref, win_ref, w2_ref, a2r_ref, a2i_ref, wc_ref, cb_ref, d_ref,
               wglu_ref, bglu_ref, wout_ref, o_ref, sre, sim, st_re, st_im):
    tm = S5_T * BATCH
    mh = S5_PAIRS * BATCH

    @pl.when(pl.program_id(0) == 0)
    def _():
        st_re[...] = jnp.zeros_like(st_re)
        st_im[...] = jnp.zeros_like(st_im)

    x4 = jnp.swapaxes(x_ref[...], 0, 1).reshape(S5_PAIRS, 2, BATCH, D_MODEL)
    x3 = jnp.concatenate([x4[:, 0], x4[:, 1]], axis=0)
    mod = mod_ref[...]
    shift = mod[:, :D_MODEL]
    scale = mod[:, D_MODEL:2 * D_MODEL]
    gate = mod[:, 2 * D_MODEL:]
    ms = jnp.mean(x3 * x3, axis=-1, keepdims=True)
    h3 = (x3 * lax.rsqrt(ms + EPS)) * (g_ref[...] * (1.0 + scale))[None] + shift[None]
    h = h3.reshape(tm, D_MODEL).astype(BF16)
    uz = jnp.dot(h, win_ref[...], preferred_element_type=F32)
    u = uz[:, :D_MODEL]
    z = uz[:, D_MODEL:]
    ub = u.astype(BF16)
    ue = ub[:mh]
    uo = ub[mh:]

    for c in range(N_CH):
        cols = slice(c * S5_CH, (c + 1) * S5_CH)
        lhs = jnp.concatenate([ue[:, cols], uo[:, cols]], axis=1)
        p = jnp.dot(lhs, w2_ref[c], preferred_element_type=F32)
        sre[c, BATCH:, :] = p[:, :CH_STATES]
        sim[c, BATCH:, :] = p[:, CH_STATES:]

    for c in range(N_CH):
        a2r = a2r_ref[c]
        a2i = a2i_ref[c]
        sr = st_re[c]
        si = st_im[c]
        sre[c, 0:BATCH, :] = sr
        sim[c, 0:BATCH, :] = si
        for m in range(S5_PAIRS):
            rows = pl.ds((m + 1) * BATCH, BATCH)
            sr, si = (a2r * sr - a2i * si + sre[c, rows, :],
                      a2r * si + a2i * sr + sim[c, rows, :])
            sre[c, rows, :] = sr
            sim[c, rows, :] = si
        st_re[c] = sr
        st_im[c] = si

    y_even, y_odd = [], []
    for c in range(N_CH):
        res = (jnp.dot(sre[c].astype(BF16), wc_ref[c, :CH_STATES, :], preferred_element_type=F32)
               + jnp.dot(sim[c].astype(BF16), wc_ref[c, CH_STATES:, :], preferred_element_type=F32))
        direct = jnp.dot(ue[:, c * S5_CH:(c + 1) * S5_CH], cb_ref[c], preferred_element_type=F32)
        y_odd.append(res[BATCH:, :S5_CH])
        y_even.append(res[:mh, S5_CH:] + direct)
    y = jnp.concatenate([jnp.concatenate(y_even, axis=1), jnp.concatenate(y_odd, axis=1)], axis=0)
    y = y + d_ref[...] * u
    y = jax.nn.gelu(y)
    gl = jnp.dot(y.astype(BF16), wglu_ref[...], preferred_element_type=F32) + bglu_ref[...]
    y = y * jax.nn.sigmoid(gl)
    y = y * (z * jax.nn.sigmoid(z))
    o = jnp.dot(y.astype(BF16), wout_ref[...], preferred_element_type=F32)
    out3 = x3 + gate[None] * o.reshape(S5_T, BATCH, D_MODEL)
    out3 = jnp.stack([out3[:S5_PAIRS], out3[S5_PAIRS:]], axis=1).reshape(S5_T, BATCH, D_MODEL)
    o_ref[...] = jnp.swapaxes(out3, 0, 1)


def _s5_layer(x, mod, g, w_in, w2, a2r, a2i, wc, cb, dvec, w_glu, b_glu, w_out):
    buf_rows = (S5_PAIRS + 1) * BATCH
    return pl.pallas_call(
        _s5_kernel,
        grid=(SEQ // S5_T,),
        in_specs=[
            pl.BlockSpec((BATCH, S5_T, D_MODEL), lambda i: (0, i, 0)),
            _const_spec((BATCH, 3 * D_MODEL)),
            _const_spec((1, D_MODEL)),
            _const_spec((D_MODEL, 2 * D_MODEL)),
            _const_spec((N_CH, 2 * S5_CH, 2 * CH_STATES)),
            _const_spec((N_CH, BATCH, CH_STATES)),
            _const_spec((N_CH, BATCH, CH_STATES)),
            _const_spec((N_CH, 2 * CH_STATES, 2 * S5_CH)),
            _const_spec((N_CH, S5_CH, S5_CH)),
            _const_spec((1, D_MODEL)),
            _const_spec((D_MODEL, D_MODEL)),
            _const_spec((1, D_MODEL)),
            _const_spec((D_MODEL, D_MODEL)),
        ],
        out_specs=pl.BlockSpec((BATCH, S5_T, D_MODEL), lambda i: (0, i, 0)),
        out_shape=jax.ShapeDtypeStruct((BATCH, SEQ, D_MODEL), F32),
        scratch_shapes=[
            pltpu.VMEM((N_CH, buf_rows, CH_STATES), F32),
            pltpu.VMEM((N_CH, buf_rows, CH_STATES), F32),
            pltpu.VMEM((N_CH, BATCH, CH_STATES), F32),
            pltpu.VMEM((N_CH, BATCH, CH_STATES), F32),
        ],
        compiler_params=_cparams(("arbitrary",)),
        name="s5_layer",
    )(x, mod, g, w_in, w2, a2r, a2i, wc, cb, dvec, w_glu, b_glu, w_out)


def _s5_params(log_dt, a_re, a_im, b_re, b_im, c_re, c_im):
    dt = jnp.exp(log_dt)[:, None]
    mag = jnp.exp(a_re * dt)
    ar, ai = mag * jnp.cos(a_im * dt), mag * jnp.sin(a_im * dt)
    den = a_re * a_re + a_im * a_im
    nr = ar - 1.0
    coef_r = (nr * a_re + ai * a_im) / den
    coef_i = (ai * a_re - nr * a_im) / den
    bb_r = coef_r[..., None] * b_re - coef_i[..., None] * b_im
    bb_i = coef_r[..., None] * b_im + coef_i[..., None] * b_re
    a2r, a2i = ar * ar - ai * ai, 2.0 * ar * ai
    ab_r = ar[..., None] * bb_r - ai[..., None] * bb_i
    ab_i = ar[..., None] * bb_i + ai[..., None] * bb_r
    ca_r = c_re * ar[:, None, :] - c_im * ai[:, None, :]
    ca_i = c_re * ai[:, None, :] + c_im * ar[:, None, :]
    hi = lax.Precision.HIGHEST
    cb = (jnp.einsum('gcp,gpk->gck', c_re, bb_r, precision=hi)
          - jnp.einsum('gcp,gpk->gck', c_im, bb_i, precision=hi))

    def block_diag(t):
        r, q = t.shape[1], t.shape[2]
        t = t.reshape(N_CH, CH_GROUPS * r, q)
        rows_blk = jnp.arange(CH_GROUPS * r) // r
        cols_blk = jnp.arange(CH_GROUPS * q) // q
        return jnp.where(rows_blk[:, None] == cols_blk[None, :], jnp.tile(t, (1, 1, CH_GROUPS)), 0.0)

    def in_rows(b_r, b_i):
        return jnp.concatenate([block_diag(b_r.transpose(0, 2, 1)), block_diag(b_i.transpose(0, 2, 1))],
                               axis=2)

    def out_cols(c_r, c_i):
        return jnp.concatenate([block_diag(c_r.transpose(0, 2, 1)), block_diag(-c_i.transpose(0, 2, 1))],
                               axis=1)

    w2 = jnp.concatenate([in_rows(ab_r, ab_i), in_rows(bb_r, bb_i)], axis=1).astype(BF16)
    wc = jnp.concatenate([out_cols(c_re, c_im), out_cols(ca_r, ca_i)], axis=2).astype(BF16)
    cbp = block_diag(cb.transpose(0, 2, 1)).astype(BF16)

    def rows(a):
        return jnp.broadcast_to(a.reshape(N_CH, 1, CH_STATES), (N_CH, BATCH, CH_STATES))

    return w2, wc, cbp, rows(a2r), rows(a2i)


def _log_sigmoid(x):
    return jnp.minimum(x, 0.0) - jnp.log1p(jnp.exp(-jnp.abs(x)))


def _proj_kernel(x_ref, mkv_ref, mb_ref, gkv_ref, gb_ref, wk_ref, wvt_ref, wf_ref, fb_ref,
                 wq_ref, wz_ref, kng_ref, qng_ref, part_ref,
                 kaug_ref, qaug_ref, vt_ref, sz_ref, ft_ref, carry_ref):
    tm = PROJ_TM

    @pl.when(pl.program_id(1) == 0)
    def _():
        carry_ref[...] = jnp.zeros_like(carry_ref)

    x = x_ref[...]
    xn = x * lax.rsqrt(jnp.mean(x * x, axis=-1, keepdims=True) + EPS)
    mkv = mkv_ref[...]
    h2 = xn * (gkv_ref[...] * (1.0 + mkv[:, D_MODEL:])) + mkv[:, :D_MODEL]
    mb = mb_ref[...]
    h3 = xn * (gb_ref[...] * (1.0 + mb[:, D_MODEL:2 * D_MODEL])) + mb[:, :D_MODEL]
    h2b = h2.astype(BF16)
    h3b = h3.astype(BF16)
    trans_b = (((1,), (1,)), ((), ()))

    f = jnp.dot(h2b, wf_ref[...], preferred_element_type=F32) + fb_ref[...]
    ls = _log_sigmoid(f)
    ri = lax.broadcasted_iota(jnp.int32, (tm, tm), 0)
    ci = lax.broadcasted_iota(jnp.int32, (tm, tm), 1)
    tri = jnp.where(ci <= ri, 1.0, 0.0).astype(BF16)
    l_hi = ls.astype(BF16)
    l_lo = (ls - l_hi.astype(F32)).astype(BF16)
    fcum = (jnp.dot(tri, l_hi, preferred_element_type=F32)
            + jnp.dot(tri, l_lo, preferred_element_type=F32)) + carry_ref[...]
    carry_ref[...] = fcum[tm - 1:tm, :]

    f2 = fcum * LOG2E
    fct = f2.T
    for h in range(N_HEADS):
        ft_ref[h] = fct[3 * h:3 * h + 1, :]
    n_hi, n_mid, n_lo = _split3(-f2)
    part = part_ref[...]
    f_parts = jnp.where(part == 0, n_hi, jnp.where(part == 1, n_mid, n_lo))

    lane = lax.broadcasted_iota(jnp.int32, (tm, LANES), 1)
    low = lane < HEAD_DIM
    ones_cols = jnp.where(lane < BIAS_LANE + 3, 1.0, 0.0)
    kng = kng_ref[...]
    qng = qng_ref[...]

    def pair_scale(sq):
        ss_a = jnp.sum(jnp.where(low, sq, 0.0), axis=-1, keepdims=True)
        ss_b = jnp.sum(jnp.where(low, 0.0, sq), axis=-1, keepdims=True)
        return lax.rsqrt(jnp.where(low, ss_a, ss_b) * (1.0 / HEAD_DIM) + EPS)

    heads_per_chunk = MXU_DIM // HEAD_DIM
    pad_row = lax.broadcasted_iota(jnp.int32, (heads_per_chunk, V_ROWS - HEAD_DIM, ATT_T), 1)
    ones_rows = jnp.where(pad_row == 0, 1.0, 0.0).astype(BF16)
    for c in range(D_MODEL // MXU_DIM):
        cols = slice(c * MXU_DIM, (c + 1) * MXU_DIM)
        hs = slice(c * heads_per_chunk, (c + 1) * heads_per_chunk)
        k = jnp.dot(h2b, wk_ref[:, cols], preferred_element_type=F32)
        q = jnp.dot(h3b, wq_ref[:, cols], preferred_element_type=F32)
        for lp in range(MXU_DIM // LANES):
            kp = k[:, lp * LANES:(lp + 1) * LANES]
            qp = q[:, lp * LANES:(lp + 1) * LANES]
            knp = (kp * pair_scale(kp * kp)) * kng
            qnp = (qp * pair_scale(qp * qp)) * qng
            for half in range(2):
                h = c * heads_per_chunk + 2 * lp + half
                ka = knp if half == 0 else pltpu.roll(knp, HEAD_DIM, 1)
                qa = qnp if half == 0 else pltpu.roll(qnp, HEAD_DIM, 1)
                bias = pltpu.roll(f_parts, BIAS_LANE - 3 * h, 1)
                bias = jnp.where(lane < BIAS_LANE + 3, bias, 0.0)
                kaug_ref[h] = jnp.where(low, ka, bias).astype(BF16)
                qaug_ref[h] = jnp.where(low, qa, ones_cols).astype(BF16)

        vt = lax.dot_general(wvt_ref[cols, :], h2b, trans_b, preferred_element_type=F32)
        vt3 = vt.reshape(heads_per_chunk, HEAD_DIM, tm).astype(BF16)
        for kt in range(KT_PER_PROJ):
            vt_ref[hs, kt, 0:HEAD_DIM, :] = vt3[:, :, kt * ATT_T:(kt + 1) * ATT_T]
            vt_ref[hs, kt, HEAD_DIM:, :] = ones_rows
        z = jnp.dot(h3b, wz_ref[:, cols], preferred_element_type=F32)
        sz_ref[:, cols] = (z * jax.nn.sigmoid(z)).astype(BF16)


def _fox_proj(x1, mkv, mb, gkv, gb, wk, wvt, wf, fb, wq, wz, kng, qng, part):
    tm = PROJ_TM
    row = lambda b, t: (b, t, 0)
    per_b = lambda b, t: (b, 0, 0)
    return pl.pallas_call(
        _proj_kernel,
        grid=(BATCH, SEQ // tm),
        in_specs=[
            pl.BlockSpec((None, tm, D_MODEL), row),
            pl.BlockSpec((None, 1, 2 * D_MODEL), per_b),
            pl.BlockSpec((None, 1, 3 * D_MODEL), per_b),
            _const_spec((1, D_MODEL)),
            _const_spec((1, D_MODEL)),
            _col_block_spec(0),
            _const_spec((D_MODEL, D_MODEL)),
            _const_spec((D_MODEL, LANES)),
            _const_spec((1, LANES)),
            _col_block_spec(0),
            _col_block_spec(1),
            _const_spec((1, LANES)),
            _const_spec((1, LANES)),
            _const_spec((1, LANES)),
        ],
        out_specs=[
            pl.BlockSpec((None, N_HEADS, tm, LANES), lambda b, t: (b, 0, t, 0)),
            pl.BlockSpec((None, N_HEADS, tm, LANES), lambda b, t: (b, 0, t, 0)),
            pl.BlockSpec((None, N_HEADS, KT_PER_PROJ, V_ROWS, ATT_T), lambda b, t: (b, 0, t, 0, 0)),
            pl.BlockSpec((None, tm, D_MODEL), lambda b, t: (b, t, 0)),
            pl.BlockSpec((None, N_HEADS, 1, tm), lambda b, t: (b, 0, 0, t)),
        ],
        out_shape=[
            jax.ShapeDtypeStruct((BATCH, N_HEADS, SEQ, LANES), BF16),
            jax.ShapeDtypeStruct((BATCH, N_HEADS, SEQ, LANES), BF16),
            jax.ShapeDtypeStruct((BATCH, N_HEADS, N_KT, V_ROWS, ATT_T), BF16),
            jax.ShapeDtypeStruct((BATCH, SEQ, D_MODEL), BF16),
            jax.ShapeDtypeStruct((BATCH, N_HEADS, 1, SEQ), F32),
        ],
        scratch_shapes=[pltpu.VMEM((1, LANES), F32)],
        compiler_params=_cparams(("arbitrary", "arbitrary")),
        name="fox_proj",
    )(x1, mkv, mb, gkv, gb, wk, wvt, wf, fb, wq, wz, kng, qng, part)


def _attn_kernel(bounded_ref, q_ref, k_ref, v_ref, fq_ref, sz_ref, x_ref, mb_ref, w_ref, o_ref,
                 acc_ref, m_ref, ot_ref, s_ref, p_ref):
    t = ATT_T
    trans_b = (((1,), (1,)), ((), ()))
    ki = lax.broadcasted_iota(jnp.int32, (t, t), 0)
    qq = lax.broadcasted_iota(jnp.int32, (t, t), 1)
    visible = ki <= qq
    heads = [(h, h) for h in range(N_HEADS)]

    def query_tile(sub):
        qi = pl.program_id(1) * ATT_Q_SUB + sub
        cols = slice(sub * t, (sub + 1) * t)

        def qk(h, kj, diagonal):
            k = k_ref[h, pl.ds(pl.multiple_of(kj * t, t), t), :]
            s = lax.dot_general(k, q_ref[h, cols, :], trans_b, preferred_element_type=F32)
            return jnp.where(visible, s, -jnp.inf) if diagonal else s

        def finish():
            for i, h in heads:
                a = acc_ref[i]
                ot_ref[h * HEAD_DIM:(h + 1) * HEAD_DIM, cols] = (
                    a[0:HEAD_DIM] * (1.0 / a[HEAD_DIM:HEAD_DIM + 1]))

        def bounded_blocks(kjs, diagonal):
            for n, kj in enumerate(kjs):
                for i, h in heads:
                    p_ref[n * N_HEADS + i] = jnp.exp2(qk(h, kj, diagonal) + fq_ref[h, :, cols]).astype(BF16)
            for i, h in heads:
                acc_ref[i] += sum(jnp.dot(v_ref[h, kj], p_ref[n * N_HEADS + i], preferred_element_type=F32)
                                  for n, kj in enumerate(kjs))

        def attend_bounded():
            def pair_body(j, _):
                bounded_blocks([2 * j, 2 * j + 1], False)
                return 0

            lax.fori_loop(0, qi // 2, pair_body, 0)

            @pl.when(qi % 2 == 1)
            def _():
                bounded_blocks([qi - 1], False)

            bounded_blocks([qi], True)
            finish()

        def running_max_blocks(kj, diagonal):
            s_max = []
            for i, h in heads:
                s = qk(h, kj, diagonal)
                s_ref[i] = s
                s_max.append(jnp.max(s, axis=0, keepdims=True))
            for (i, h), sm in zip(heads, s_max):
                fq = fq_ref[h, :, cols]
                m_old = m_ref[i]
                m_new = jnp.maximum(m_old, sm + fq)
                alpha = jnp.exp2(m_old - m_new)
                p = jnp.exp2(s_ref[i] + (fq - m_new)).astype(BF16)
                acc_ref[i] = alpha * acc_ref[i] + jnp.dot(v_ref[h, kj], p, preferred_element_type=F32)
                m_ref[i] = m_new

        def attend_running_max():
            def k_body(kj, _):
                running_max_blocks(kj, False)
                return 0

            lax.fori_loop(0, qi, k_body, 0)
            running_max_blocks(qi, True)
            finish()

        for i, _h in heads:
            acc_ref[i] = jnp.zeros((V_ROWS, t), F32)

        @pl.when(bounded_ref[0] == 1)
        def _():
            attend_bounded()

        @pl.when(bounded_ref[0] != 1)
        def _():
            for i, _h in heads:
                m_ref[i] = jnp.full((1, t), -jnp.inf, F32)
            attend_running_max()

    for sub in range(ATT_Q_SUB):
        query_tile(sub)

    y = (ot_ref[...].T * sz_ref[...].astype(F32)).astype(BF16)
    out = jnp.dot(y, w_ref[...], preferred_element_type=F32)
    gate = mb_ref[...][:, 2 * D_MODEL:]
    o_ref[...] = x_ref[...] + gate * out


def _fox_attn(bounded, qaug, kaug, vt, ft, sz, x1, mb, w_out):
    t = ATT_T
    tq = ATT_Q_SUB * t
    return pl.pallas_call(
        _attn_kernel,
        grid=(BATCH, SEQ // tq),
        in_specs=[
            pl.BlockSpec(memory_space=pltpu.SMEM),
            pl.BlockSpec((None, N_HEADS, tq, LANES), lambda b, i: (b, 0, i, 0)),
            pl.BlockSpec((None, N_HEADS, SEQ, LANES), lambda b, i: (b, 0, 0, 0)),
            pl.BlockSpec((None, N_HEADS, N_KT, V_ROWS, t), lambda b, i: (b, 0, 0, 0, 0)),
            pl.BlockSpec((None, N_HEADS, 1, tq), lambda b, i: (b, 0, 0, i)),
            pl.BlockSpec((None, tq, D_MODEL), lambda b, i: (b, i, 0)),
            pl.BlockSpec((None, tq, D_MODEL), lambda b, i: (b, i, 0)),
            pl.BlockSpec((None, 1, 3 * D_MODEL), lambda b, i: (b, 0, 0)),
            _const_spec((D_MODEL, D_MODEL)),
        ],
        out_specs=pl.BlockSpec((None, tq, D_MODEL), lambda b, i: (b, i, 0)),
        out_shape=jax.ShapeDtypeStruct((BATCH, SEQ, D_MODEL), F32),
        scratch_shapes=[
            pltpu.VMEM((N_HEADS, V_ROWS, t), F32),
            pltpu.VMEM((N_HEADS, 1, t), F32),
            pltpu.VMEM((D_MODEL, tq), F32),
            pltpu.VMEM((N_HEADS, t, t), F32),
            pltpu.VMEM((2 * N_HEADS, t, t), BF16),
        ],
        compiler_params=_cparams(("arbitrary", "arbitrary"), ATTN_VMEM_LIMIT_BYTES),
        name="fox_attn",
    )(bounded, qaug, kaug, vt, ft, sz, x1, mb, w_out)


def kernel(x, c, a_norm_g, a_mod_w, a_mod_b, a_w_in, a_log_dt, a_A_re, a_A_im, a_B_re, a_B_im,
           a_C_re, a_C_im, a_D, a_w_glu, a_b_glu, a_w_out, kv_norm_g, kv_mod_w, kv_mod_b, kv_w,
           kv_f_bias, k_norm_g, b_norm_g, b_mod_w, b_mod_b, b_w_in, q_norm_g, b_w_out):
    assert x.shape == (BATCH, SEQ, D_MODEL) and a_mod_w.shape[0] == 1 and b_mod_w.shape[0] == 1
    aw = N_HEADS * HEAD_DIM

    mod_a, mod_kv, mod_b = _modulation(c, a_mod_w[0], a_mod_b[0], kv_mod_w, kv_mod_b,
                                       b_mod_w[0], b_mod_b[0])

    w2, wc, cb, a2r, a2i = _s5_params(a_log_dt[0], a_A_re[0], a_A_im[0], a_B_re[0], a_B_im[0],
                                      a_C_re[0], a_C_im[0])
    x1 = _s5_layer(x, mod_a, a_norm_g[0].reshape(1, D_MODEL), a_w_in[0].astype(BF16), w2, a2r, a2i,
                   wc, cb, a_D[0].reshape(1, D_MODEL), a_w_glu[0].astype(BF16),
                   a_b_glu[0].reshape(1, D_MODEL), a_w_out[0].astype(BF16))

    kvb = kv_w.astype(BF16)
    wvt = kv_w[:, aw:2 * aw].T.astype(BF16)
    wf = jnp.pad(jnp.repeat(kv_w[:, 2 * aw:], 3, axis=1),
                 ((0, 0), (0, LANES - 3 * N_HEADS))).astype(BF16)
    fb = jnp.pad(jnp.repeat(kv_f_bias, 3), (0, LANES - 3 * N_HEADS)).reshape(1, LANES)
    part = (jnp.arange(LANES, dtype=jnp.int32) % 3).reshape(1, LANES)
    wqz = b_w_in[0].astype(BF16)
    kng = jnp.tile(k_norm_g, 2).reshape(1, LANES)
    qng = (jnp.tile(q_norm_g[0], 2) * (HEAD_DIM ** -0.5 * LOG2E)).reshape(1, LANES)
    mkv3 = mod_kv.reshape(BATCH, 1, 2 * D_MODEL)
    mb3 = mod_b.reshape(BATCH, 1, 3 * D_MODEL)
    kaug, qaug, vt, sz, ft = _fox_proj(x1, mkv3, mb3, kv_norm_g.reshape(1, D_MODEL),
                                       b_norm_g[0].reshape(1, D_MODEL), kvb, wvt, wf, fb,
                                       wqz, wqz, kng, qng, part)
    qk_bound = HEAD_DIM * jnp.max(jnp.abs(kng)) * jnp.max(jnp.abs(qng))
    bounded = (qk_bound <= MAX_UNSTABILISED_LOG2).astype(jnp.int32).reshape(1)
    return _fox_attn(bounded, qaug, kaug, vt, ft, sz, x1, mb3, b_w_out[0].astype(BF16))
```

```python
import jax
import jax.numpy as jnp
from jax import lax
from jax.experimental import pallas as pl
from jax.experimental.pallas import tpu as pltpu

D_MODEL = 1024
BATCH = 8
SEQ = 2048
GROUP = 16
N_GROUPS = D_MODEL // GROUP
STATE = 64
N_STATES = N_GROUPS * STATE
N_HEADS = 16
HEAD_DIM = 64
EPS = 1e-6

F32 = jnp.float32
BF16 = jnp.bfloat16

SUBLANES = 8
LANES = 128
MXU_DIM = 256
VMEM_LIMIT_BYTES = 56 * 1024 * 1024
ATTN_VMEM_LIMIT_BYTES = 58 * 1024 * 1024

S5_T = 64
S5_PAIRS = S5_T // 2
S5_CH = LANES
N_CH = D_MODEL // S5_CH
CH_GROUPS = S5_CH // GROUP
CH_STATES = CH_GROUPS * STATE
PROJ_TM = 512
ATT_T = 256
ATT_Q_SUB = 2
N_KT = SEQ // ATT_T
KT_PER_PROJ = PROJ_TM // ATT_T
MAX_UNSTABILISED_LOG2 = 64.0
V_ROWS = HEAD_DIM + 16
BIAS_LANE = HEAD_DIM
MOD_BN = 512
MOD_KV_BLOCKS = 2 * D_MODEL // MOD_BN
LOG2E = 1.4426950408889634


def _cparams(sem, vmem_limit_bytes=None):
    return pltpu.CompilerParams(dimension_semantics=sem,
                                vmem_limit_bytes=vmem_limit_bytes or VMEM_LIMIT_BYTES)


def _const_spec(shape):
    nd = len(shape)
    return pl.BlockSpec(shape, lambda *_: (0,) * nd, pipeline_mode=pl.Buffered(1))


def _col_block_spec(j):
    return pl.BlockSpec((D_MODEL, D_MODEL), lambda *_: (0, j), pipeline_mode=pl.Buffered(1))


def _split3(x):
    hi = x.astype(BF16).astype(F32)
    r = x - hi
    mid = r.astype(BF16).astype(F32)
    lo = (r - mid).astype(BF16).astype(F32)
    return hi, mid, lo


def _mod_kernel(c_ref, wa_ref, ba_ref, wkv_ref, bkv_ref, wb_ref, bb_ref, oa_ref, okv_ref, ob_ref):
    c = c_ref[...]
    s = (c * jax.nn.sigmoid(c)).astype(BF16)

    def site(w_ref, b_ref, o_ref):
        o_ref[...] = jnp.dot(s, w_ref[...].astype(BF16), preferred_element_type=F32) + b_ref[...]

    site(wa_ref, ba_ref, oa_ref)
    site(wb_ref, bb_ref, ob_ref)

    @pl.when(pl.program_id(0) < MOD_KV_BLOCKS)
    def _():
        site(wkv_ref, bkv_ref, okv_ref)


def _modulation(c, wa, ba, wkv, bkv, wb, bb):
    col = lambda j: (0, j)
    col_kv = lambda j: (0, jnp.minimum(j, MOD_KV_BLOCKS - 1))
    n3, n2 = 3 * D_MODEL, 2 * D_MODEL
    return pl.pallas_call(
        _mod_kernel,
        grid=(n3 // MOD_BN,),
        in_specs=[
            pl.BlockSpec((BATCH, D_MODEL), lambda j: (0, 0)),
            pl.BlockSpec((D_MODEL, MOD_BN), col),
            pl.BlockSpec((1, MOD_BN), col),
            pl.BlockSpec((D_MODEL, MOD_BN), col_kv),
            pl.BlockSpec((1, MOD_BN), col_kv),
            pl.BlockSpec((D_MODEL, MOD_BN), col),
            pl.BlockSpec((1, MOD_BN), col),
        ],
        out_specs=[
            pl.BlockSpec((BATCH, MOD_BN), col),
            pl.BlockSpec((BATCH, MOD_BN), col_kv),
            pl.BlockSpec((BATCH, MOD_BN), col),
        ],
        out_shape=[
            jax.ShapeDtypeStruct((BATCH, n3), F32),
            jax.ShapeDtypeStruct((BATCH, n2), F32),
            jax.ShapeDtypeStruct((BATCH, n3), F32),
        ],
        compiler_params=_cparams(("arbitrary",)),
        name="modulation",
    )(c, wa, ba.reshape(1, n3), wkv, bkv.reshape(1, n2), wb, bb.reshape(1, n3))


def _s5_kernel(x_ref, mod_ref, g_ref, win_ref, w2_ref, a2r_ref, a2i_ref, wc_ref, cb_ref, d_ref,
               wglu_ref, bglu_ref, wout_ref, o_ref, sre, sim, st_re, st_im):
    tm = S5_T * BATCH
    mh = S5_PAIRS * BATCH

    @pl.when(pl.program_id(0) == 0)
    def _():
        st_re[...] = jnp.zeros_like(st_re)
        st_im[...] = jnp.zeros_like(st_im)

    x4 = jnp.swapaxes(x_ref[...], 0, 1).reshape(S5_PAIRS, 2, BATCH, D_MODEL)
    x3 = jnp.concatenate([x4[:, 0], x4[:, 1]], axis=0)
    mod = mod_ref[...]
    shift = mod[:, :D_MODEL]
    scale = mod[:, D_MODEL:2 * D_MODEL]
    gate = mod[:, 2 * D_MODEL:]
    ms = jnp.mean(x3 * x3, axis=-1, keepdims=True)
    h3 = (x3 * lax.rsqrt(ms + EPS)) * (g_ref[...] * (1.0 + scale))[None] + shift[None]
    h = h3.reshape(tm, D_MODEL).astype(BF16)
    uz = jnp.dot(h, win_ref[...], preferred_element_type=F32)
    u = uz[:, :D_MODEL]
    z = uz[:, D_MODEL:]
    ub = u.astype(BF16)
    ue = ub[:mh]
    uo = ub[mh:]

    for c in range(N_CH):
        cols = slice(c * S5_CH, (c + 1) * S5_CH)
        lhs = jnp.concatenate([ue[:, cols], uo[:, cols]], axis=1)
        p = jnp.dot(lhs, w2_ref[c], preferred_element_type=F32)
        sre[c, BATCH:, :] = p[:, :CH_STATES]
        sim[c, BATCH:, :] = p[:, CH_STATES:]

    for c in range(N_CH):
        a2r = a2r_ref[c]
        a2i = a2i_ref[c]
        sr = st_re[c]
        si = st_im[c]
        sre[c, 0:BATCH, :] = sr
        sim[c, 0:BATCH, :] = si
        for m in range(S5_PAIRS):
            rows = pl.ds((m + 1) * BATCH, BATCH)
            sr, si = (a2r * sr - a2i * si + sre[c, rows, :],
                      a2r * si + a2i * sr + sim[c, rows, :])
            sre[c, rows, :] = sr
            sim[c, rows, :] = si
        st_re[c] = sr
        st_im[c] = si

    y_even, y_odd = [], []
    for c in range(N_CH):
        res = (jnp.dot(sre[c].astype(BF16), wc_ref[c, :CH_STATES, :], preferred_element_type=F32)
               + jnp.dot(sim[c].astype(BF16), wc_ref[c, CH_STATES:, :], preferred_element_type=F32))
        direct = jnp.dot(ue[:, c * S5_CH:(c + 1) * S5_CH], cb_ref[c], preferred_element_type=F32)
        y_odd.append(res[BATCH:, :S5_CH])
        y_even.append(res[:mh, S5_CH:] + direct)
    y = jnp.concatenate([jnp.concatenate(y_even, axis=1), jnp.concatenate(y_odd, axis=1)], axis=0)
    y = y + d_ref[...] * u
    y = jax.nn.gelu(y)
    gl = jnp.dot(y.astype(BF16), wglu_ref[...], preferred_element_type=F32) + bglu_ref[...]
    y = y * jax.nn.sigmoid(gl)
    y = y * (z * jax.nn.sigmoid(z))
    o = jnp.dot(y.astype(BF16), wout_ref[...], preferred_element_type=F32)
    out3 = x3 + gate[None] * o.reshape(S5_T, BATCH, D_MODEL)
    out3 = jnp.stack([out3[:S5_PAIRS], out3[S5_PAIRS:]], axis=1).reshape(S5_T, BATCH, D_MODEL)
    o_ref[...] = jnp.swapaxes(out3, 0, 1)


def _s5_layer(x, mod, g, w_in, w2, a2r, a2i, wc, cb, dvec, w_glu, b_glu, w_out):
    buf_rows = (S5_PAIRS + 1) * BATCH
    return pl.pallas_call(
        _s5_kernel,
        grid=(SEQ // S5_T,),
        in_specs=[
            pl.BlockSpec((BATCH, S5_T, D_MODEL), lambda i: (0, i, 0)),
            _const_spec((BATCH, 3 * D_MODEL)),
            _const_spec((1, D_MODEL)),
            _const_spec((D_MODEL, 2 * D_MODEL)),
            _const_spec((N_CH, 2 * S5_CH, 2 * CH_STATES)),
            _const_spec((N_CH, BATCH, CH_STATES)),
            _const_spec((N_CH, BATCH, CH_STATES)),
            _const_spec((N_CH, 2 * CH_STATES, 2 * S5_CH)),
            _const_spec((N_CH, S5_CH, S5_CH)),
            _const_spec((1, D_MODEL)),
            _const_spec((D_MODEL, D_MODEL)),
            _const_spec((1, D_MODEL)),
            _const_spec((D_MODEL, D_MODEL)),
        ],
        out_specs=pl.BlockSpec((BATCH, S5_T, D_MODEL), lambda i: (0, i, 0)),
        out_shape=jax.ShapeDtypeStruct((BATCH, SEQ, D_MODEL), F32),
        scratch_shapes=[
            pltpu.VMEM((N_CH, buf_rows, CH_STATES), F32),
            pltpu.VMEM((N_CH, buf_rows, CH_STATES), F32),
            pltpu.VMEM((N_CH, BATCH, CH_STATES), F32),
            pltpu.VMEM((N_CH, BATCH, CH_STATES), F32),
        ],
        compiler_params=_cparams(("arbitrary",)),
        name="s5_layer",
    )(x, mod, g, w_in, w2, a2r, a2i, wc, cb, dvec, w_glu, b_glu, w_out)


def _s5_params(log_dt, a_re, a_im, b_re, b_im, c_re, c_im):
    dt = jnp.exp(log_dt)[:, None]
    mag = jnp.exp(a_re * dt)
    ar, ai = mag * jnp.cos(a_im * dt), mag * jnp.sin(a_im * dt)
    den = a_re * a_re + a_im * a_im
    nr = ar - 1.0
    coef_r = (nr * a_re + ai * a_im) / den
    coef_i = (ai * a_re - nr * a_im) / den
    bb_r = coef_r[..., None] * b_re - coef_i[..., None] * b_im
    bb_i = coef_r[..., None] * b_im + coef_i[..., None] * b_re
    a2r, a2i = ar * ar - ai * ai, 2.0 * ar * ai
    ab_r = ar[..., None] * bb_r - ai[..., None] * bb_i
    ab_i = ar[..., None] * bb_i + ai[..., None] * bb_r
    ca_r = c_re * ar[:, None, :] - c_im * ai[:, None, :]
    ca_i = c_re * ai[:, None, :] + c_im * ar[:, None, :]
    hi = lax.Precision.HIGHEST
    cb = (jnp.einsum('gcp,gpk->gck', c_re, bb_r, precision=hi)
          - jnp.einsum('gcp,gpk->gck', c_im, bb_i, precision=hi))

    def block_diag(t):
        r, q = t.shape[1], t.shape[2]
        t = t.reshape(N_CH, CH_GROUPS * r, q)
        rows_blk = jnp.arange(CH_GROUPS * r) // r
        cols_blk = jnp.arange(CH_GROUPS * q) // q
        return jnp.where(rows_blk[:, None] == cols_blk[None, :], jnp.tile(t, (1, 1, CH_GROUPS)), 0.0)

    def in_rows(b_r, b_i):
        return jnp.concatenate([block_diag(b_r.transpose(0, 2, 1)), block_diag(b_i.transpose(0, 2, 1))],
                               axis=2)

    def out_cols(c_r, c_i):
        return jnp.concatenate([block_diag(c_r.transpose(0, 2, 1)), block_diag(-c_i.transpose(0, 2, 1))],
                               axis=1)

    w2 = jnp.concatenate([in_rows(ab_r, ab_i), in_rows(bb_r, bb_i)], axis=1).astype(BF16)
    wc = jnp.concatenate([out_cols(c_re, c_im), out_cols(ca_r, ca_i)], axis=2).astype(BF16)
    cbp = block_diag(cb.transpose(0, 2, 1)).astype(BF16)

    def rows(a):
        return jnp.broadcast_to(a.reshape(N_CH, 1, CH_STATES), (N_CH, BATCH, CH_STATES))

    return w2, wc, cbp, rows(a2r), rows(a2i)


def _log_sigmoid(x):
    return jnp.minimum(x, 0.0) - jnp.log1p(jnp.exp(-jnp.abs(x)))


def _proj_kernel(x_ref, mkv_ref, mb_ref, gkv_ref, gb_ref, wk_ref, wvt_ref, wf_ref, fb_ref,
                 wq_ref, wz_ref, kng_ref, qng_ref, part_ref,
                 kaug_ref, qaug_ref, vt_ref, sz_ref, ft_ref, carry_ref):
    tm = PROJ_TM

    @pl.when(pl.program_id(1) == 0)
    def _():
        carry_ref[...] = jnp.zeros_like(carry_ref)

    x = x_ref[...]
    xn = x * lax.rsqrt(jnp.mean(x * x, axis=-1, keepdims=True) + EPS)
    mkv = mkv_ref[...]
    h2 = xn * (gkv_ref[...] * (1.0 + mkv[:, D_MODEL:])) + mkv[:, :D_MODEL]
    mb = mb_ref[...]
    h3 = xn * (gb_ref[...] * (1.0 + mb[:, D_MODEL:2 * D_MODEL])) + mb[:, :D_MODEL]
    h2b = h2.astype(BF16)
    h3b = h3.astype(BF16)
    trans_b = (((1,), (1,)), ((), ()))

    f = jnp.dot(h2b, wf_ref[...], preferred_element_type=F32) + fb_ref[...]
    ls = _log_sigmoid(f)
    ri = lax.broadcasted_iota(jnp.int32, (tm, tm), 0)
    ci = lax.broadcasted_iota(jnp.int32, (tm, tm), 1)
    tri = jnp.where(ci <= ri, 1.0, 0.0).astype(BF16)
    l_hi = ls.astype(BF16)
    l_lo = (ls - l_hi.astype(F32)).astype(BF16)
    fcum = (jnp.dot(tri, l_hi, preferred_element_type=F32)
            + jnp.dot(tri, l_lo, preferred_element_type=F32)) + carry_ref[...]
    carry_ref[...] = fcum[tm - 1:tm, :]

    f2 = fcum * LOG2E
    fct = f2.T
    for h in range(N_HEADS):
        ft_ref[h] = fct[3 * h:3 * h + 1, :]
    n_hi, n_mid, n_lo = _split3(-f2)
    part = part_ref[...]
    f_parts = jnp.where(part == 0, n_hi, jnp.where(part == 1, n_mid, n_lo))

    lane = lax.broadcasted_iota(jnp.int32, (tm, LANES), 1)
    low = lane < HEAD_DIM
    ones_cols = jnp.where(lane < BIAS_LANE + 3, 1.0, 0.0)
    kng = kng_ref[...]
    qng = qng_ref[...]

    def pair_scale(sq):
        ss_a = jnp.sum(jnp.where(low, sq, 0.0), axis=-1, keepdims=True)
        ss_b = jnp.sum(jnp.where(low, 0.0, sq), axis=-1, keepdims=True)
        return lax.rsqrt(jnp.where(low, ss_a, ss_b) * (1.0 / HEAD_DIM) + EPS)

    heads_per_chunk = MXU_DIM // HEAD_DIM
    pad_row = lax.broadcasted_iota(jnp.int32, (heads_per_chunk, V_ROWS - HEAD_DIM, ATT_T), 1)
    ones_rows = jnp.where(pad_row == 0, 1.0, 0.0).astype(BF16)
    for c in range(D_MODEL // MXU_DIM):
        cols = slice(c * MXU_DIM, (c + 1) * MXU_DIM)
        hs = slice(c * heads_per_chunk, (c + 1) * heads_per_chunk)
        k = jnp.dot(h2b, wk_ref[:, cols], preferred_element_type=F32)
        q = jnp.dot(h3b, wq_ref[:, cols], preferred_element_type=F32)
        for lp in range(MXU_DIM // LANES):
            kp = k[:, lp * LANES:(lp + 1) * LANES]
            qp = q[:, lp * LANES:(lp + 1) * LANES]
            knp = (kp * pair_scale(kp * kp)) * kng
            qnp = (qp * pair_scale(qp * qp)) * qng
            for half in range(2):
                h = c * heads_per_chunk + 2 * lp + half
                ka = knp if half == 0 else pltpu.roll(knp, HEAD_DIM, 1)
                qa = qnp if half == 0 else pltpu.roll(qnp, HEAD_DIM, 1)
                bias = pltpu.roll(f_parts, BIAS_LANE - 3 * h, 1)
                bias = jnp.where(lane < BIAS_LANE + 3, bias, 0.0)
                kaug_ref[h] = jnp.where(low, ka, bias).astype(BF16)
                qaug_ref[h] = jnp.where(low, qa, ones_cols).astype(BF16)

        vt = lax.dot_general(wvt_ref[cols, :], h2b, trans_b, preferred_element_type=F32)
        vt3 = vt.reshape(heads_per_chunk, HEAD_DIM, tm).astype(BF16)
        for kt in range(KT_PER_PROJ):
            vt_ref[hs, kt, 0:HEAD_DIM, :] = vt3[:, :, kt * ATT_T:(kt + 1) * ATT_T]
            vt_ref[hs, kt, HEAD_DIM:, :] = ones_rows
        z = jnp.dot(h3b, wz_ref[:, cols], preferred_element_type=F32)
        sz_ref[:, cols] = (z * jax.nn.sigmoid(z)).astype(BF16)


def _fox_proj(x1, mkv, mb, gkv, gb, wk, wvt, wf, fb, wq, wz, kng, qng, part):
    tm = PROJ_TM
    row = lambda b, t: (b, t, 0)
    per_b = lambda b, t: (b, 0, 0)
    return pl.pallas_call(
        _proj_kernel,
        grid=(BATCH, SEQ // tm),
        in_specs=[
            pl.BlockSpec((None, tm, D_MODEL), row),
            pl.BlockSpec((None, 1, 2 * D_MODEL), per_b),
            pl.BlockSpec((None, 1, 3 * D_MODEL), per_b),
            _const_spec((1, D_MODEL)),
            _const_spec((1, D_MODEL)),
            _col_block_spec(0),
            _const_spec((D_MODEL, D_MODEL)),
            _const_spec((D_MODEL, LANES)),
            _const_spec((1, LANES)),
            _col_block_spec(0),
            _col_block_spec(1),
            _const_spec((1, LANES)),
            _const_spec((1, LANES)),
            _const_spec((1, LANES)),
        ],
        out_specs=[
            pl.BlockSpec((None, N_HEADS, tm, LANES), lambda b, t: (b, 0, t, 0)),
            pl.BlockSpec((None, N_HEADS, tm, LANES), lambda b, t: (b, 0, t, 0)),
            pl.BlockSpec((None, N_HEADS, KT_PER_PROJ, V_ROWS, ATT_T), lambda b, t: (b, 0, t, 0, 0)),
            pl.BlockSpec((None, tm, D_MODEL), lambda b, t: (b, t, 0)),
            pl.BlockSpec((None, N_HEADS, 1, tm), lambda b, t: (b, 0, 0, t)),
        ],
        out_shape=[
            jax.ShapeDtypeStruct((BATCH, N_HEADS, SEQ, LANES), BF16),
            jax.ShapeDtypeStruct((BATCH, N_HEADS, SEQ, LANES), BF16),
            jax.ShapeDtypeStruct((BATCH, N_HEADS, N_KT, V_ROWS, ATT_T), BF16),
            jax.ShapeDtypeStruct((BATCH, SEQ, D_MODEL), BF16),
            jax.ShapeDtypeStruct((BATCH, N_HEADS, 1, SEQ), F32),
        ],
        scratch_shapes=[pltpu.VMEM((1, LANES), F32)],
        compiler_params=_cparams(("arbitrary", "arbitrary")),
        name="fox_proj",
    )(x1, mkv, mb, gkv, gb, wk, wvt, wf, fb, wq, wz, kng, qng, part)


def _attn_kernel(kng_ref, qng_ref, q_ref, k_ref, v_ref, fq_ref, sz_ref, x_ref, mb_ref, w_ref, o_ref,
                 acc_ref, m_ref, ot_ref, s_ref, p_ref):
    t = ATT_T
    trans_b = (((1,), (1,)), ((), ()))
    ki = lax.broadcasted_iota(jnp.int32, (t, t), 0)
    qq = lax.broadcasted_iota(jnp.int32, (t, t), 1)
    visible = ki <= qq
    heads = [(h, h) for h in range(N_HEADS)]
    qk_bound = HEAD_DIM * jnp.max(jnp.abs(kng_ref[...])) * jnp.max(jnp.abs(qng_ref[...]))
    bounded = qk_bound <= MAX_UNSTABILISED_LOG2

    def query_tile(sub):
        qi = pl.program_id(1) * ATT_Q_SUB + sub
        cols = slice(sub * t, (sub + 1) * t)

        def qk(h, kj, diagonal):
            k = k_ref[h, pl.ds(pl.multiple_of(kj * t, t), t), :]
            s = lax.dot_general(k, q_ref[h, cols, :], trans_b, preferred_element_type=F32)
            return jnp.where(visible, s, -jnp.inf) if diagonal else s

        def finish():
            for i, h in heads:
                a = acc_ref[i]
                ot_ref[h * HEAD_DIM:(h + 1) * HEAD_DIM, cols] = (
                    a[0:HEAD_DIM] * (1.0 / a[HEAD_DIM:HEAD_DIM + 1]))

        def bounded_blocks(kjs, diagonal):
            for n, kj in enumerate(kjs):
                for i, h in heads:
                    p_ref[n * N_HEADS + i] = jnp.exp2(qk(h, kj, diagonal) + fq_ref[h, :, cols]).astype(BF16)
            for i, h in heads:
                acc_ref[i] += sum(jnp.dot(v_ref[h, kj], p_ref[n * N_HEADS + i], preferred_element_type=F32)
                                  for n, kj in enumerate(kjs))

        def attend_bounded():
            def pair_body(j, _):
                bounded_blocks([2 * j, 2 * j + 1], False)
                return 0

            lax.fori_loop(0, qi // 2, pair_body, 0)

            @pl.when(qi % 2 == 1)
            def _():
                bounded_blocks([qi - 1], False)

            bounded_blocks([qi], True)
            finish()

        def running_max_blocks(kj, diagonal):
            s_max = []
            for i, h in heads:
                s = qk(h, kj, diagonal)
                s_ref[i] = s
                s_max.append(jnp.max(s, axis=0, keepdims=True))
            for (i, h), sm in zip(heads, s_max):
                fq = fq_ref[h, :, cols]
                m_old = m_ref[i]
                m_new = jnp.maximum(m_old, sm + fq)
                alpha = jnp.exp2(m_old - m_new)
                p = jnp.exp2(s_ref[i] + (fq - m_new)).astype(BF16)
                acc_ref[i] = alpha * acc_ref[i] + jnp.dot(v_ref[h, kj], p, preferred_element_type=F32)
                m_ref[i] = m_new

        def attend_running_max():
            def k_body(kj, _):
                running_max_blocks(kj, False)
                return 0

            lax.fori_loop(0, qi, k_body, 0)
            running_max_blocks(qi, True)
            finish()

        for i, _h in heads:
            acc_ref[i] = jnp.zeros((V_ROWS, t), F32)

        @pl.when(bounded)
        def _():
            attend_bounded()

        @pl.when(jnp.logical_not(bounded))
        def _():
            for i, _h in heads:
                m_ref[i] = jnp.full((1, t), -jnp.inf, F32)
            attend_running_max()

    for sub in range(ATT_Q_SUB):
        query_tile(sub)

    y = (ot_ref[...].T * sz_ref[...].astype(F32)).astype(BF16)
    out = jnp.dot(y, w_ref[...], preferred_element_type=F32)
    gate = mb_ref[...][:, 2 * D_MODEL:]
    o_ref[...] = x_ref[...] + gate * out


def _fox_attn(kng, qng, qaug, kaug, vt, ft, sz, x1, mb, w_out):
    t = ATT_T
    tq = ATT_Q_SUB * t
    return pl.pallas_call(
        _attn_kernel,
        grid=(BATCH, SEQ // tq),
        in_specs=[
            _const_spec((1, LANES)),
            _const_spec((1, LANES)),
            pl.BlockSpec((None, N_HEADS, tq, LANES), lambda b, i: (b, 0, i, 0)),
            pl.BlockSpec((None, N_HEADS, SEQ, LANES), lambda b, i: (b, 0, 0, 0)),
            pl.BlockSpec((None, N_HEADS, N_KT, V_ROWS, t), lambda b, i: (b, 0, 0, 0, 0)),
            pl.BlockSpec((None, N_HEADS, 1, tq), lambda b, i: (b, 0, 0, i)),
            pl.BlockSpec((None, tq, D_MODEL), lambda b, i: (b, i, 0)),
            pl.BlockSpec((None, tq, D_MODEL), lambda b, i: (b, i, 0)),
            pl.BlockSpec((None, 1, 3 * D_MODEL), lambda b, i: (b, 0, 0)),
            _const_spec((D_MODEL, D_MODEL)),
        ],
        out_specs=pl.BlockSpec((None, tq, D_MODEL), lambda b, i: (b, i, 0)),
        out_shape=jax.ShapeDtypeStruct((BATCH, SEQ, D_MODEL), F32),
        scratch_shapes=[
            pltpu.VMEM((N_HEADS, V_ROWS, t), F32),
            pltpu.VMEM((N_HEADS, 1, t), F32),
            pltpu.VMEM((D_MODEL, tq), F32),
            pltpu.VMEM((N_HEADS, t, t), F32),
            pltpu.VMEM((2 * N_HEADS, t, t), BF16),
        ],
        compiler_params=_cparams(("arbitrary", "arbitrary"), ATTN_VMEM_LIMIT_BYTES),
        name="fox_attn",
    )(kng, qng, qaug, kaug, vt, ft, sz, x1, mb, w_out)


def kernel(x, c, a_norm_g, a_mod_w, a_mod_b, a_w_in, a_log_dt, a_A_re, a_A_im, a_B_re, a_B_im,
           a_C_re, a_C_im, a_D, a_w_glu, a_b_glu, a_w_out, kv_norm_g, kv_mod_w, kv_mod_b, kv_w,
           kv_f_bias, k_norm_g, b_norm_g, b_mod_w, b_mod_b, b_w_in, q_norm_g, b_w_out):
    assert x.shape == (BATCH, SEQ, D_MODEL) and a_mod_w.shape[0] == 1 and b_mod_w.shape[0] == 1
    aw = N_HEADS * HEAD_DIM

    mod_a, mod_kv, mod_b = _modulation(c, a_mod_w[0], a_mod_b[0], kv_mod_w, kv_mod_b,
                                       b_mod_w[0], b_mod_b[0])

    w2, wc, cb, a2r, a2i = _s5_params(a_log_dt[0], a_A_re[0], a_A_im[0], a_B_re[0], a_B_im[0],
                                      a_C_re[0], a_C_im[0])
    x1 = _s5_layer(x, mod_a, a_norm_g[0].reshape(1, D_MODEL), a_w_in[0].astype(BF16), w2, a2r, a2i,
                   wc, cb, a_D[0].reshape(1, D_MODEL), a_w_glu[0].astype(BF16),
                   a_b_glu[0].reshape(1, D_MODEL), a_w_out[0].astype(BF16))

    kvb = kv_w.astype(BF16)
    wvt = kv_w[:, aw:2 * aw].T.astype(BF16)
    wf = jnp.pad(jnp.repeat(kv_w[:, 2 * aw:], 3, axis=1),
                 ((0, 0), (0, LANES - 3 * N_HEADS))).astype(BF16)
    fb = jnp.pad(jnp.repeat(kv_f_bias, 3), (0, LANES - 3 * N_HEADS)).reshape(1, LANES)
    part = (jnp.arange(LANES, dtype=jnp.int32) % 3).reshape(1, LANES)
    wqz = b_w_in[0].astype(BF16)
    kng = jnp.tile(k_norm_g, 2).reshape(1, LANES)
    qng = (jnp.tile(q_norm_g[0], 2) * (HEAD_DIM ** -0.5 * LOG2E)).reshape(1, LANES)
    mkv3 = mod_kv.reshape(BATCH, 1, 2 * D_MODEL)
    mb3 = mod_b.reshape(BATCH, 1, 3 * D_MODEL)
    kaug, qaug, vt, sz, ft = _fox_proj(x1, mkv3, mb3, kv_norm_g.reshape(1, D_MODEL),
                                       b_norm_g[0].reshape(1, D_MODEL), kvb, wvt, wf, fb,
                                       wqz, wqz, kng, qng, part)
    return _fox_attn(kng, qng, qaug, kaug, vt, ft, sz, x1, mb3, b_w_out[0].astype(BF16))
```

```python
import jax
import jax.numpy as jnp
from jax import lax
from jax.experimental import pallas as pl
from jax.experimental.pallas import tpu as pltpu

D_MODEL = 1024
BATCH = 8
SEQ = 2048
GROUP = 16
N_GROUPS = D_MODEL // GROUP
STATE = 64
N_STATES = N_GROUPS * STATE
N_HEADS = 16
HEAD_DIM = 64
EPS = 1e-6

F32 = jnp.float32
BF16 = jnp.bfloat16

SUBLANES = 8
LANES = 128
MXU_DIM = 256
VMEM_LIMIT_BYTES = 56 * 1024 * 1024

S5_T = 64
S5_PAIRS = S5_T // 2
S5_CH = LANES
N_CH = D_MODEL // S5_CH
CH_GROUPS = S5_CH // GROUP
CH_STATES = CH_GROUPS * STATE
PROJ_TM = 512
ATT_T = 256
ATT_Q_SUB = 2
RUNNING_MAX_HEADS = 8
N_KT = SEQ // ATT_T
KT_PER_PROJ = PROJ_TM // ATT_T
MAX_UNSTABILISED_LOG2 = 64.0
V_ROWS = HEAD_DIM + 16
BIAS_LANE = HEAD_DIM
MOD_BN = 512
MOD_KV_BLOCKS = 2 * D_MODEL // MOD_BN
LOG2E = 1.4426950408889634


def _cparams(sem):
    return pltpu.CompilerParams(dimension_semantics=sem, vmem_limit_bytes=VMEM_LIMIT_BYTES)


def _const_spec(shape):
    nd = len(shape)
    return pl.BlockSpec(shape, lambda *_: (0,) * nd, pipeline_mode=pl.Buffered(1))


def _col_block_spec(j):
    return pl.BlockSpec((D_MODEL, D_MODEL), lambda *_: (0, j), pipeline_mode=pl.Buffered(1))


def _split3(x):
    hi = x.astype(BF16).astype(F32)
    r = x - hi
    mid = r.astype(BF16).astype(F32)
    lo = (r - mid).astype(BF16).astype(F32)
    return hi, mid, lo


def _mod_kernel(c_ref, wa_ref, ba_ref, wkv_ref, bkv_ref, wb_ref, bb_ref, oa_ref, okv_ref, ob_ref):
    c = c_ref[...]
    s = (c * jax.nn.sigmoid(c)).astype(BF16)

    def site(w_ref, b_ref, o_ref):
        o_ref[...] = jnp.dot(s, w_ref[...].astype(BF16), preferred_element_type=F32) + b_ref[...]

    site(wa_ref, ba_ref, oa_ref)
    site(wb_ref, bb_ref, ob_ref)

    @pl.when(pl.program_id(0) < MOD_KV_BLOCKS)
    def _():
        site(wkv_ref, bkv_ref, okv_ref)


def _modulation(c, wa, ba, wkv, bkv, wb, bb):
    col = lambda j: (0, j)
    col_kv = lambda j: (0, jnp.minimum(j, MOD_KV_BLOCKS - 1))
    n3, n2 = 3 * D_MODEL, 2 * D_MODEL
    return pl.pallas_call(
        _mod_kernel,
        grid=(n3 // MOD_BN,),
        in_specs=[
            pl.BlockSpec((BATCH, D_MODEL), lambda j: (0, 0)),
            pl.BlockSpec((D_MODEL, MOD_BN), col),
            pl.BlockSpec((1, MOD_BN), col),
            pl.BlockSpec((D_MODEL, MOD_BN), col_kv),
            pl.BlockSpec((1, MOD_BN), col_kv),
            pl.BlockSpec((D_MODEL, MOD_BN), col),
            pl.BlockSpec((1, MOD_BN), col),
        ],
        out_specs=[
            pl.BlockSpec((BATCH, MOD_BN), col),
            pl.BlockSpec((BATCH, MOD_BN), col_kv),
            pl.BlockSpec((BATCH, MOD_BN), col),
        ],
        out_shape=[
            jax.ShapeDtypeStruct((BATCH, n3), F32),
            jax.ShapeDtypeStruct((BATCH, n2), F32),
            jax.ShapeDtypeStruct((BATCH, n3), F32),
        ],
        compiler_params=_cparams(("arbitrary",)),
        name="modulation",
    )(c, wa, ba.reshape(1, n3), wkv, bkv.reshape(1, n2), wb, bb.reshape(1, n3))


def _s5_kernel(x_ref, mod_ref, g_ref, win_ref, w2_ref, a2r_ref, a2i_ref, wc_ref, cb_ref, d_ref,
               wglu_ref, bglu_ref, wout_ref, o_ref, sre, sim, st_re, st_im):
    tm = S5_T * BATCH
    mh = S5_PAIRS * BATCH

    @pl.when(pl.program_id(0) == 0)
    def _():
        st_re[...] = jnp.zeros_like(st_re)
        st_im[...] = jnp.zeros_like(st_im)

    x4 = jnp.swapaxes(x_ref[...], 0, 1).reshape(S5_PAIRS, 2, BATCH, D_MODEL)
    x3 = jnp.concatenate([x4[:, 0], x4[:, 1]], axis=0)
    mod = mod_ref[...]
    shift = mod[:, :D_MODEL]
    scale = mod[:, D_MODEL:2 * D_MODEL]
    gate = mod[:, 2 * D_MODEL:]
    ms = jnp.mean(x3 * x3, axis=-1, keepdims=True)
    h3 = (x3 * lax.rsqrt(ms + EPS)) * (g_ref[...] * (1.0 + scale))[None] + shift[None]
    h = h3.reshape(tm, D_MODEL).astype(BF16)
    uz = jnp.dot(h, win_ref[...], preferred_element_type=F32)
    u = uz[:, :D_MODEL]
    z = uz[:, D_MODEL:]
    ub = u.astype(BF16)
    ue = ub[:mh]
    uo = ub[mh:]

    for c in range(N_CH):
        cols = slice(c * S5_CH, (c + 1) * S5_CH)
        lhs = jnp.concatenate([ue[:, cols], uo[:, cols]], axis=1)
        p = jnp.dot(lhs, w2_ref[c], preferred_element_type=F32)
        sre[c, BATCH:, :] = p[:, :CH_STATES]
        sim[c, BATCH:, :] = p[:, CH_STATES:]

    for c in range(N_CH):
        a2r = a2r_ref[c]
        a2i = a2i_ref[c]
        sr = st_re[c]
        si = st_im[c]
        sre[c, 0:BATCH, :] = sr
        sim[c, 0:BATCH, :] = si
        for m in range(S5_PAIRS):
            rows = pl.ds((m + 1) * BATCH, BATCH)
            sr, si = (a2r * sr - a2i * si + sre[c, rows, :],
                      a2r * si + a2i * sr + sim[c, rows, :])
            sre[c, rows, :] = sr
            sim[c, rows, :] = si
        st_re[c] = sr
        st_im[c] = si

    y_even, y_odd = [], []
    for c in range(N_CH):
        res = (jnp.dot(sre[c].astype(BF16), wc_ref[c, :CH_STATES, :], preferred_element_type=F32)
               + jnp.dot(sim[c].astype(BF16), wc_ref[c, CH_STATES:, :], preferred_element_type=F32))
        direct = jnp.dot(ue[:, c * S5_CH:(c + 1) * S5_CH], cb_ref[c], preferred_element_type=F32)
        y_odd.append(res[BATCH:, :S5_CH])
        y_even.append(res[:mh, S5_CH:] + direct)
    y = jnp.concatenate([jnp.concatenate(y_even, axis=1), jnp.concatenate(y_odd, axis=1)], axis=0)
    y = y + d_ref[...] * u
    y = jax.nn.gelu(y)
    gl = jnp.dot(y.astype(BF16), wglu_ref[...], preferred_element_type=F32) + bglu_ref[...]
    y = y * jax.nn.sigmoid(gl)
    y = y * (z * jax.nn.sigmoid(z))
    o = jnp.dot(y.astype(BF16), wout_ref[...], preferred_element_type=F32)
    out3 = x3 + gate[None] * o.reshape(S5_T, BATCH, D_MODEL)
    out3 = jnp.stack([out3[:S5_PAIRS], out3[S5_PAIRS:]], axis=1).reshape(S5_T, BATCH, D_MODEL)
    o_ref[...] = jnp.swapaxes(out3, 0, 1)


def _s5_layer(x, mod, g, w_in, w2, a2r, a2i, wc, cb, dvec, w_glu, b_glu, w_out):
    buf_rows = (S5_PAIRS + 1) * BATCH
    return pl.pallas_call(
        _s5_kernel,
        grid=(SEQ // S5_T,),
        in_specs=[
            pl.BlockSpec((BATCH, S5_T, D_MODEL), lambda i: (0, i, 0)),
            _const_spec((BATCH, 3 * D_MODEL)),
            _const_spec((1, D_MODEL)),
            _const_spec((D_MODEL, 2 * D_MODEL)),
            _const_spec((N_CH, 2 * S5_CH, 2 * CH_STATES)),
            _const_spec((N_CH, BATCH, CH_STATES)),
            _const_spec((N_CH, BATCH, CH_STATES)),
            _const_spec((N_CH, 2 * CH_STATES, 2 * S5_CH)),
            _const_spec((N_CH, S5_CH, S5_CH)),
            _const_spec((1, D_MODEL)),
            _const_spec((D_MODEL, D_MODEL)),
            _const_spec((1, D_MODEL)),
            _const_spec((D_MODEL, D_MODEL)),
        ],
        out_specs=pl.BlockSpec((BATCH, S5_T, D_MODEL), lambda i: (0, i, 0)),
        out_shape=jax.ShapeDtypeStruct((BATCH, SEQ, D_MODEL), F32),
        scratch_shapes=[
            pltpu.VMEM((N_CH, buf_rows, CH_STATES), F32),
            pltpu.VMEM((N_CH, buf_rows, CH_STATES), F32),
            pltpu.VMEM((N_CH, BATCH, CH_STATES), F32),
            pltpu.VMEM((N_CH, BATCH, CH_STATES), F32),
        ],
        compiler_params=_cparams(("arbitrary",)),
        name="s5_layer",
    )(x, mod, g, w_in, w2, a2r, a2i, wc, cb, dvec, w_glu, b_glu, w_out)


def _s5_params(log_dt, a_re, a_im, b_re, b_im, c_re, c_im):
    dt = jnp.exp(log_dt)[:, None]
    mag = jnp.exp(a_re * dt)
    ar, ai = mag * jnp.cos(a_im * dt), mag * jnp.sin(a_im * dt)
    den = a_re * a_re + a_im * a_im
    nr = ar - 1.0
    coef_r = (nr * a_re + ai * a_im) / den
    coef_i = (ai * a_re - nr * a_im) / den
    bb_r = coef_r[..., None] * b_re - coef_i[..., None] * b_im
    bb_i = coef_r[..., None] * b_im + coef_i[..., None] * b_re
    a2r, a2i = ar * ar - ai * ai, 2.0 * ar * ai
    ab_r = ar[..., None] * bb_r - ai[..., None] * bb_i
    ab_i = ar[..., None] * bb_i + ai[..., None] * bb_r
    ca_r = c_re * ar[:, None, :] - c_im * ai[:, None, :]
    ca_i = c_re * ai[:, None, :] + c_im * ar[:, None, :]
    hi = lax.Precision.HIGHEST
    cb = (jnp.einsum('gcp,gpk->gck', c_re, bb_r, precision=hi)
          - jnp.einsum('gcp,gpk->gck', c_im, bb_i, precision=hi))

    def block_diag(t):
        r, q = t.shape[1], t.shape[2]
        t = t.reshape(N_CH, CH_GROUPS * r, q)
        rows_blk = jnp.arange(CH_GROUPS * r) // r
        cols_blk = jnp.arange(CH_GROUPS * q) // q
        return jnp.where(rows_blk[:, None] == cols_blk[None, :], jnp.tile(t, (1, 1, CH_GROUPS)), 0.0)

    def in_rows(b_r, b_i):
        return jnp.concatenate([block_diag(b_r.transpose(0, 2, 1)), block_diag(b_i.transpose(0, 2, 1))],
                               axis=2)

    def out_cols(c_r, c_i):
        return jnp.concatenate([block_diag(c_r.transpose(0, 2, 1)), block_diag(-c_i.transpose(0, 2, 1))],
                               axis=1)

    w2 = jnp.concatenate([in_rows(ab_r, ab_i), in_rows(bb_r, bb_i)], axis=1).astype(BF16)
    wc = jnp.concatenate([out_cols(c_re, c_im), out_cols(ca_r, ca_i)], axis=2).astype(BF16)
    cbp = block_diag(cb.transpose(0, 2, 1)).astype(BF16)

    def rows(a):
        return jnp.broadcast_to(a.reshape(N_CH, 1, CH_STATES), (N_CH, BATCH, CH_STATES))

    return w2, wc, cbp, rows(a2r), rows(a2i)


def _log_sigmoid(x):
    return jnp.minimum(x, 0.0) - jnp.log1p(jnp.exp(-jnp.abs(x)))


def _proj_kernel(x_ref, mkv_ref, mb_ref, gkv_ref, gb_ref, wk_ref, wvt_ref, wf_ref, fb_ref,
                 wq_ref, wz_ref, kng_ref, qng_ref, part_ref,
                 kaug_ref, qaug_ref, vt_ref, sz_ref, ft_ref, carry_ref):
    tm = PROJ_TM

    @pl.when(pl.program_id(1) == 0)
    def _():
        carry_ref[...] = jnp.zeros_like(carry_ref)

    x = x_ref[...]
    xn = x * lax.rsqrt(jnp.mean(x * x, axis=-1, keepdims=True) + EPS)
    mkv = mkv_ref[...]
    h2 = xn * (gkv_ref[...] * (1.0 + mkv[:, D_MODEL:])) + mkv[:, :D_MODEL]
    mb = mb_ref[...]
    h3 = xn * (gb_ref[...] * (1.0 + mb[:, D_MODEL:2 * D_MODEL])) + mb[:, :D_MODEL]
    h2b = h2.astype(BF16)
    h3b = h3.astype(BF16)
    trans_b = (((1,), (1,)), ((), ()))

    f = jnp.dot(h2b, wf_ref[...], preferred_element_type=F32) + fb_ref[...]
    ls = _log_sigmoid(f)
    ri = lax.broadcasted_iota(jnp.int32, (tm, tm), 0)
    ci = lax.broadcasted_iota(jnp.int32, (tm, tm), 1)
    tri = jnp.where(ci <= ri, 1.0, 0.0).astype(BF16)
    l_hi = ls.astype(BF16)
    l_lo = (ls - l_hi.astype(F32)).astype(BF16)
    fcum = (jnp.dot(tri, l_hi, preferred_element_type=F32)
            + jnp.dot(tri, l_lo, preferred_element_type=F32)) + carry_ref[...]
    carry_ref[...] = fcum[tm - 1:tm, :]

    f2 = fcum * LOG2E
    fct = f2.T
    for h in range(N_HEADS):
        ft_ref[h] = fct[3 * h:3 * h + 1, :]
    n_hi, n_mid, n_lo = _split3(-f2)
    part = part_ref[...]
    f_parts = jnp.where(part == 0, n_hi, jnp.where(part == 1, n_mid, n_lo))

    lane = lax.broadcasted_iota(jnp.int32, (tm, LANES), 1)
    low = lane < HEAD_DIM
    ones_cols = jnp.where(lane < BIAS_LANE + 3, 1.0, 0.0)
    kng = kng_ref[...]
    qng = qng_ref[...]

    def pair_scale(sq):
        ss_a = jnp.sum(jnp.where(low, sq, 0.0), axis=-1, keepdims=True)
        ss_b = jnp.sum(jnp.where(low, 0.0, sq), axis=-1, keepdims=True)
        return lax.rsqrt(jnp.where(low, ss_a, ss_b) * (1.0 / HEAD_DIM) + EPS)

    heads_per_chunk = MXU_DIM // HEAD_DIM
    pad_row = lax.broadcasted_iota(jnp.int32, (heads_per_chunk, V_ROWS - HEAD_DIM, ATT_T), 1)
    ones_rows = jnp.where(pad_row == 0, 1.0, 0.0).astype(BF16)
    for c in range(D_MODEL // MXU_DIM):
        cols = slice(c * MXU_DIM, (c + 1) * MXU_DIM)
        hs = slice(c * heads_per_chunk, (c + 1) * heads_per_chunk)
        k = jnp.dot(h2b, wk_ref[:, cols], preferred_element_type=F32)
        q = jnp.dot(h3b, wq_ref[:, cols], preferred_element_type=F32)
        for lp in range(MXU_DIM // LANES):
            kp = k[:, lp * LANES:(lp + 1) * LANES]
            qp = q[:, lp * LANES:(lp + 1) * LANES]
            knp = (kp * pair_scale(kp * kp)) * kng
            qnp = (qp * pair_scale(qp * qp)) * qng
            for half in range(2):
                h = c * heads_per_chunk + 2 * lp + half
                ka = knp if half == 0 else pltpu.roll(knp, HEAD_DIM, 1)
                qa = qnp if half == 0 else pltpu.roll(qnp, HEAD_DIM, 1)
                bias = pltpu.roll(f_parts, BIAS_LANE - 3 * h, 1)
                bias = jnp.where(lane < BIAS_LANE + 3, bias, 0.0)
                kaug_ref[h] = jnp.where(low, ka, bias).astype(BF16)
                qaug_ref[h] = jnp.where(low, qa, ones_cols).astype(BF16)

        vt = lax.dot_general(wvt_ref[cols, :], h2b, trans_b, preferred_element_type=F32)
        vt3 = vt.reshape(heads_per_chunk, HEAD_DIM, tm).astype(BF16)
        for kt in range(KT_PER_PROJ):
            vt_ref[hs, kt, 0:HEAD_DIM, :] = vt3[:, :, kt * ATT_T:(kt + 1) * ATT_T]
            vt_ref[hs, kt, HEAD_DIM:, :] = ones_rows
        z = jnp.dot(h3b, wz_ref[:, cols], preferred_element_type=F32)
        sz_ref[:, cols] = (z * jax.nn.sigmoid(z)).astype(BF16)


def _fox_proj(x1, mkv, mb, gkv, gb, wk, wvt, wf, fb, wq, wz, kng, qng, part):
    tm = PROJ_TM
    row = lambda b, t: (b, t, 0)
    per_b = lambda b, t: (b, 0, 0)
    return pl.pallas_call(
        _proj_kernel,
        grid=(BATCH, SEQ // tm),
        in_specs=[
            pl.BlockSpec((None, tm, D_MODEL), row),
            pl.BlockSpec((None, 1, 2 * D_MODEL), per_b),
            pl.BlockSpec((None, 1, 3 * D_MODEL), per_b),
            _const_spec((1, D_MODEL)),
            _const_spec((1, D_MODEL)),
            _col_block_spec(0),
            _const_spec((D_MODEL, D_MODEL)),
            _const_spec((D_MODEL, LANES)),
            _const_spec((1, LANES)),
            _col_block_spec(0),
            _col_block_spec(1),
            _const_spec((1, LANES)),
            _const_spec((1, LANES)),
            _const_spec((1, LANES)),
        ],
        out_specs=[
            pl.BlockSpec((None, N_HEADS, tm, LANES), lambda b, t: (b, 0, t, 0)),
            pl.BlockSpec((None, N_HEADS, tm, LANES), lambda b, t: (b, 0, t, 0)),
            pl.BlockSpec((None, N_HEADS, KT_PER_PROJ, V_ROWS, ATT_T), lambda b, t: (b, 0, t, 0, 0)),
            pl.BlockSpec((None, tm, D_MODEL), lambda b, t: (b, t, 0)),
            pl.BlockSpec((None, N_HEADS, 1, tm), lambda b, t: (b, 0, 0, t)),
        ],
        out_shape=[
            jax.ShapeDtypeStruct((BATCH, N_HEADS, SEQ, LANES), BF16),
            jax.ShapeDtypeStruct((BATCH, N_HEADS, SEQ, LANES), BF16),
            jax.ShapeDtypeStruct((BATCH, N_HEADS, N_KT, V_ROWS, ATT_T), BF16),
            jax.ShapeDtypeStruct((BATCH, SEQ, D_MODEL), BF16),
            jax.ShapeDtypeStruct((BATCH, N_HEADS, 1, SEQ), F32),
        ],
        scratch_shapes=[pltpu.VMEM((1, LANES), F32)],
        compiler_params=_cparams(("arbitrary", "arbitrary")),
        name="fox_proj",
    )(x1, mkv, mb, gkv, gb, wk, wvt, wf, fb, wq, wz, kng, qng, part)


def _attn_kernel(bounded_ref, q_ref, k_ref, v_ref, fq_ref, sz_ref, x_ref, mb_ref, w_ref, o_ref,
                 acc_ref, m_ref, ot_ref, s_ref, p_ref):
    t = ATT_T
    trans_b = (((1,), (1,)), ((), ()))
    ki = lax.broadcasted_iota(jnp.int32, (t, t), 0)
    qq = lax.broadcasted_iota(jnp.int32, (t, t), 1)
    visible = ki <= qq
    heads = [(h, h) for h in range(N_HEADS)]

    def query_tile(sub):
        qi = pl.program_id(1) * ATT_Q_SUB + sub
        cols = slice(sub * t, (sub + 1) * t)

        def qk(h, kj, diagonal):
            k = k_ref[h, pl.ds(pl.multiple_of(kj * t, t), t), :]
            s = lax.dot_general(k, q_ref[h, cols, :], trans_b, preferred_element_type=F32)
            return jnp.where(visible, s, -jnp.inf) if diagonal else s

        def finish():
            for i, h in heads:
                a = acc_ref[i]
                ot_ref[h * HEAD_DIM:(h + 1) * HEAD_DIM, cols] = (
                    a[0:HEAD_DIM] * (1.0 / a[HEAD_DIM:HEAD_DIM + 1]))

        def bounded_blocks(kjs, diagonal):
            for n, kj in enumerate(kjs):
                for i, h in heads:
                    p_ref[n * N_HEADS + i] = jnp.exp2(qk(h, kj, diagonal) + fq_ref[h, :, cols]).astype(BF16)
            for i, h in heads:
                acc_ref[i] += sum(jnp.dot(v_ref[h, kj], p_ref[n * N_HEADS + i], preferred_element_type=F32)
                                  for n, kj in enumerate(kjs))

        def attend_bounded():
            def pair_body(j, _):
                bounded_blocks([2 * j, 2 * j + 1], False)
                return 0

            lax.fori_loop(0, qi // 2, pair_body, 0)

            @pl.when(qi % 2 == 1)
            def _():
                bounded_blocks([qi - 1], False)

            bounded_blocks([qi], True)
            finish()

        def running_max_blocks(kj, diagonal):
            for g in range(0, N_HEADS, RUNNING_MAX_HEADS):
                group = heads[g:g + RUNNING_MAX_HEADS]
                s_max = []
                for i, h in group:
                    s = qk(h, kj, diagonal)
                    s_ref[i - g] = s
                    s_max.append(jnp.max(s, axis=0, keepdims=True))
                for (i, h), sm in zip(group, s_max):
                    fq = fq_ref[h, :, cols]
                    m_old = m_ref[i]
                    m_new = jnp.maximum(m_old, sm + fq)
                    alpha = jnp.exp2(m_old - m_new)
                    p = jnp.exp2(s_ref[i - g] + (fq - m_new)).astype(BF16)
                    acc_ref[i] = alpha * acc_ref[i] + jnp.dot(v_ref[h, kj], p, preferred_element_type=F32)
                    m_ref[i] = m_new

        def attend_running_max():
            def k_body(kj, _):
                running_max_blocks(kj, False)
                return 0

            lax.fori_loop(0, qi, k_body, 0)
            running_max_blocks(qi, True)
            finish()

        for i, _h in heads:
            acc_ref[i] = jnp.zeros((V_ROWS, t), F32)

        @pl.when(bounded_ref[0] == 1)
        def _():
            attend_bounded()

        @pl.when(bounded_ref[0] != 1)
        def _():
            for i, _h in heads:
                m_ref[i] = jnp.full((1, t), -jnp.inf, F32)
            attend_running_max()

    for sub in range(ATT_Q_SUB):
        query_tile(sub)

    y = (ot_ref[...].T * sz_ref[...].astype(F32)).astype(BF16)
    out = jnp.dot(y, w_ref[...], preferred_element_type=F32)
    gate = mb_ref[...][:, 2 * D_MODEL:]
    o_ref[...] = x_ref[...] + gate * out


def _fox_attn(bounded, qaug, kaug, vt, ft, sz, x1, mb, w_out):
    t = ATT_T
    tq = ATT_Q_SUB * t
    return pl.pallas_call(
        _attn_kernel,
        grid=(BATCH, SEQ // tq),
        in_specs=[
            pl.BlockSpec(memory_space=pltpu.SMEM),
            pl.BlockSpec((None, N_HEADS, tq, LANES), lambda b, i: (b, 0, i, 0)),
            pl.BlockSpec((None, N_HEADS, SEQ, LANES), lambda b, i: (b, 0, 0, 0)),
            pl.BlockSpec((None, N_HEADS, N_KT, V_ROWS, t), lambda b, i: (b, 0, 0, 0, 0)),
            pl.BlockSpec((None, N_HEADS, 1, tq), lambda b, i: (b, 0, 0, i)),
            pl.BlockSpec((None, tq, D_MODEL), lambda b, i: (b, i, 0)),
            pl.BlockSpec((None, tq, D_MODEL), lambda b, i: (b, i, 0)),
            pl.BlockSpec((None, 1, 3 * D_MODEL), lambda b, i: (b, 0, 0)),
            _const_spec((D_MODEL, D_MODEL)),
        ],
        out_specs=pl.BlockSpec((None, tq, D_MODEL), lambda b, i: (b, i, 0)),
        out_shape=jax.ShapeDtypeStruct((BATCH, SEQ, D_MODEL), F32),
        scratch_shapes=[
            pltpu.VMEM((N_HEADS, V_ROWS, t), F32),
            pltpu.VMEM((N_HEADS, 1, t), F32),
            pltpu.VMEM((D_MODEL, tq), F32),
            pltpu.VMEM((RUNNING_MAX_HEADS, t, t), F32),
            pltpu.VMEM((2 * N_HEADS, t, t), BF16),
        ],
        compiler_params=_cparams(("arbitrary", "arbitrary")),
        name="fox_attn",
    )(bounded, qaug, kaug, vt, ft, sz, x1, mb, w_out)


def kernel(x, c, a_norm_g, a_mod_w, a_mod_b, a_w_in, a_log_dt, a_A_re, a_A_im, a_B_re, a_B_im,
           a_C_re, a_C_im, a_D, a_w_glu, a_b_glu, a_w_out, kv_norm_g, kv_mod_w, kv_mod_b, kv_w,
           kv_f_bias, k_norm_g, b_norm_g, b_mod_w, b_mod_b, b_w_in, q_norm_g, b_w_out):
    assert x.shape == (BATCH, SEQ, D_MODEL) and a_mod_w.shape[0] == 1 and b_mod_w.shape[0] == 1
    aw = N_HEADS * HEAD_DIM

    mod_a, mod_kv, mod_b = _modulation(c, a_mod_w[0], a_mod_b[0], kv_mod_w, kv_mod_b,
                                       b_mod_w[0], b_mod_b[0])

    w2, wc, cb, a2r, a2i = _s5_params(a_log_dt[0], a_A_re[0], a_A_im[0], a_B_re[0], a_B_im[0],
                                      a_C_re[0], a_C_im[0])
    x1 = _s5_layer(x, mod_a, a_norm_g[0].reshape(1, D_MODEL), a_w_in[0].astype(BF16), w2, a2r, a2i,
                   wc, cb, a_D[0].reshape(1, D_MODEL), a_w_glu[0].astype(BF16),
                   a_b_glu[0].reshape(1, D_MODEL), a_w_out[0].astype(BF16))

    kvb = kv_w.astype(BF16)
    wvt = kv_w[:, aw:2 * aw].T.astype(BF16)
    wf = jnp.pad(jnp.repeat(kv_w[:, 2 * aw:], 3, axis=1),
                 ((0, 0), (0, LANES - 3 * N_HEADS))).astype(BF16)
    fb = jnp.pad(jnp.repeat(kv_f_bias, 3), (0, LANES - 3 * N_HEADS)).reshape(1, LANES)
    part = (jnp.arange(LANES, dtype=jnp.int32) % 3).reshape(1, LANES)
    wqz = b_w_in[0].astype(BF16)
    kng = jnp.tile(k_norm_g, 2).reshape(1, LANES)
    qng = (jnp.tile(q_norm_g[0], 2) * (HEAD_DIM ** -0.5 * LOG2E)).reshape(1, LANES)
    mkv3 = mod_kv.reshape(BATCH, 1, 2 * D_MODEL)
    mb3 = mod_b.reshape(BATCH, 1, 3 * D_MODEL)
    kaug, qaug, vt, sz, ft = _fox_proj(x1, mkv3, mb3, kv_norm_g.reshape(1, D_MODEL),
                                       b_norm_g[0].reshape(1, D_MODEL), kvb, wvt, wf, fb,
                                       wqz, wqz, kng, qng, part)
    qk_bound = HEAD_DIM * jnp.max(jnp.abs(kng)) * jnp.max(jnp.abs(qng))
    bounded = (qk_bound <= MAX_UNSTABILISED_LOG2).astype(jnp.int32).reshape(1)
    return _fox_attn(bounded, qaug, kaug, vt, ft, sz, x1, mb3, b_w_out[0].astype(BF16))
```
